```python
import jax, jax.numpy as jnp
from jax import lax
import numpy as np

D_MODEL = 1024
BATCH = 32
SEQ = 2048
DEPTH = 1

GRID_W = 64
CTX_LEN = 256
HEAD_DIM = 128
N_Q_HEADS = D_MODEL // HEAD_DIM
N_KV_HEADS = N_Q_HEADS // 4
Q_GROUP = N_Q_HEADS // N_KV_HEADS
ATTN_WIDTH = N_Q_HEADS * HEAD_DIM
KV_WIDTH = N_KV_HEADS * HEAD_DIM
POOL_WINDOWS = (2, 4, 8, 16)
N_POOL_GROUPS = len(POOL_WINDOWS)
POOL_WIDTH = D_MODEL // 2
POOL_GROUP_WIDTH = POOL_WIDTH // N_POOL_GROUPS
N_BRANCHES = 2
IN_WIDTH = ATTN_WIDTH + 2 * KV_WIDTH + POOL_WIDTH + N_BRANCHES * D_MODEL
SPLIT_POINTS = (ATTN_WIDTH, ATTN_WIDTH + KV_WIDTH, ATTN_WIDTH + 2 * KV_WIDTH,
                ATTN_WIDTH + 2 * KV_WIDTH + POOL_WIDTH)
D_FF = 4 * D_MODEL
Q_BLOCK = 128
ROPE_THETA = 10000.0
EPS = 1e-6
N_MOD = 6

kernel_name = "hybrid_gated_gqa_pool_dit_block"


def rms_norm(x, g):
    xf = x.astype(jnp.float32)
    y = xf * lax.rsqrt(jnp.mean(xf * xf, axis=-1, keepdims=True) + EPS)
    return (y * g.astype(jnp.float32)).astype(x.dtype)


def adaln(cond, w_mod, b_mod):
    m = jax.nn.silu(cond) @ w_mod + b_mod
    return jnp.split(m, N_MOD, axis=-1)


def modulate(h, shift, scale):
    return h * (1 + scale) + shift


def rotate_half_pairs(xp, ang):
    f = ang.shape[-1]
    x1, x2 = xp[..., :f], xp[..., f:]
    cos, sin = jnp.cos(ang), jnp.sin(ang)
    return jnp.concatenate([x1 * cos - x2 * sin, x1 * sin + x2 * cos], axis=-1)


def axial_rope(x, ang_row, ang_col):
    xf = x.astype(jnp.float32)
    half = HEAD_DIM // 2
    out = jnp.concatenate([rotate_half_pairs(xf[..., :half], ang_row),
                           rotate_half_pairs(xf[..., half:], ang_col)], axis=-1)
    return out.astype(x.dtype)


def to_heads(t, n_heads):
    b, n, _ = t.shape
    return t.reshape(b, n, n_heads, HEAD_DIM).transpose(0, 2, 1, 3)


def from_heads(t):
    b, h, n, d = t.shape
    return t.transpose(0, 2, 1, 3).reshape(b, n, h * d)


def split_projection(h, w_in, g_q, g_k):
    p = h @ w_in
    q, k, v, pool_in, gate_logits = jnp.split(p, SPLIT_POINTS, axis=-1)
    q = rms_norm(to_heads(q, N_Q_HEADS), g_q)
    k = rms_norm(to_heads(k, N_KV_HEADS), g_k)
    v = to_heads(v, N_KV_HEADS)
    return q, k, v, pool_in, gate_logits


def gqa_softmax(qg, k, v):
    s = jnp.einsum('bkgqd,bkmd->bkgqm', qg, k, preferred_element_type=jnp.float32)
    p = jax.nn.softmax(s * (HEAD_DIM ** -0.5), axis=-1)
    return jnp.einsum('bkgqm,bkmd->bkgqd', p.astype(v.dtype), v)


def latent_attention(q, k_lat, v_lat, k_ctx, v_ctx):
    b, _, n, _ = q.shape
    k_all = jnp.concatenate([k_ctx, k_lat], axis=2)
    v_all = jnp.concatenate([v_ctx, v_lat], axis=2)
    n_blk = n // Q_BLOCK
    qb = q.reshape(b, N_KV_HEADS, Q_GROUP, n_blk, Q_BLOCK, HEAD_DIM)
    qb = jnp.moveaxis(qb, 3, 0)
    o = lax.map(lambda blk: gqa_softmax(blk, k_all, v_all), qb)
    o = jnp.moveaxis(o, 0, 3).reshape(b, N_Q_HEADS, n, HEAD_DIM)
    return from_heads(o)


def context_attention(q, k, v):
    b, _, n, _ = q.shape
    qg = q.reshape(b, N_KV_HEADS, Q_GROUP, n, HEAD_DIM)
    o = gqa_softmax(qg, k, v).reshape(b, N_Q_HEADS, n, HEAD_DIM)
    return from_heads(o)


def multiscale_pool(u, w_grp, scale):
    b, n, _ = u.shape
    uf = u.astype(jnp.float32)
    cs = jnp.concatenate([jnp.zeros((b, 1, POOL_WIDTH), jnp.float32),
                          jnp.cumsum(uf, axis=1)], axis=1)
    t = jnp.arange(n)
    outs = []
    for gi, w in enumerate(POOL_WINDOWS):
        lo = jnp.clip(t - w // 2, 0, n)
        hi = jnp.clip(t + w // 2, 0, n)
        sl = slice(gi * POOL_GROUP_WIDTH, (gi + 1) * POOL_GROUP_WIDTH)
        csg = cs[..., sl]
        cnt = (hi - lo).astype(jnp.float32)[:, None]
        mean = (jnp.take(csg, hi, axis=1) - jnp.take(csg, lo, axis=1)) / cnt
        outs.append(mean - uf[..., sl])
    pooled = jnp.stack(outs, axis=2).astype(u.dtype)
    mixed = jnp.einsum('btgc,gcd->btgd', pooled, w_grp).reshape(b, n, POOL_WIDTH)
    return mixed * scale


def merge_branches(attn_o, pool_o, gate_logits, w_attn_up, w_pool_up, w_out):
    ya = attn_o @ w_attn_up
    yp = pool_o @ w_pool_up
    gates = jax.nn.sigmoid(gate_logits.astype(jnp.float32)).astype(ya.dtype)
    ga, gp = jnp.split(gates, N_BRANCHES, axis=-1)
    return (ga * ya + gp * yp) @ w_out


def sq_relu_mlp(h, w_ff1, w_ff2):
    return jnp.square(jax.nn.relu(h @ w_ff1)) @ w_ff2


def setup_inputs(seed: int = 0) -> dict:
    key = jax.random.key(seed)
    ks = jax.random.split(key, 24)
    f32 = jnp.float32

    def w(k, shape, fan_in):
        return jax.random.normal(k, shape, f32) * (fan_in ** -0.5)

    def gain(k, shape):
        return 1.0 + 0.05 * jax.random.normal(k, shape, f32)

    def bias(k, shape):
        return 0.02 * jax.random.normal(k, shape, f32)

    L = DEPTH
    return {
        "x": jax.random.normal(ks[0], (BATCH, SEQ, D_MODEL), f32),
        "c": jax.random.normal(ks[1], (BATCH, D_MODEL), f32),
        "ctx": jax.random.normal(ks[2], (BATCH, CTX_LEN, D_MODEL), f32),
        "c_ctx": jax.random.normal(ks[3], (D_MODEL,), f32),
        "w_mod": w(ks[4], (L, D_MODEL, N_MOD * D_MODEL), D_MODEL),
        "b_mod": bias(ks[5], (L, N_MOD * D_MODEL)),
        "g_pre_mix": gain(ks[6], (L, D_MODEL)),
        "g_post_mix": gain(ks[7], (L, D_MODEL)),
        "g_pre_mlp": gain(ks[8], (L, D_MODEL)),
        "g_post_mlp": gain(ks[9], (L, D_MODEL)),
        "w_in": w(ks[10], (L, D_MODEL, IN_WIDTH), D_MODEL),
        "b_gate": bias(ks[11], (L, N_BRANCHES * D_MODEL)),
        "g_q": gain(ks[12], (L, HEAD_DIM)),
        "g_k": gain(ks[13], (L, HEAD_DIM)),
        "w_attn_up": w(ks[14], (L, ATTN_WIDTH, D_MODEL), ATTN_WIDTH),
        "w_pool_grp": w(ks[15], (L, N_POOL_GROUPS, POOL_GROUP_WIDTH, POOL_GROUP_WIDTH), POOL_GROUP_WIDTH),
        "pool_scale": gain(ks[16], (L, POOL_WIDTH)),
        "w_pool_up": w(ks[17], (L, POOL_WIDTH, D_MODEL), POOL_WIDTH),
        "w_out": w(ks[18], (L, D_MODEL, D_MODEL), D_MODEL),
        "w_ff1": w(ks[19], (L, D_MODEL, D_FF), D_MODEL),
        "w_ff2": w(ks[20], (L, D_FF, D_MODEL), D_FF),
    }


def reference(x, c, ctx, c_ctx, w_mod, b_mod, g_pre_mix, g_post_mix, g_pre_mlp, g_post_mlp,
              w_in, b_gate, g_q, g_k, w_attn_up, w_pool_grp, pool_scale, w_pool_up, w_out,
              w_ff1, w_ff2):
    n_lat = x.shape[1]
    ROWS = n_lat // GRID_W
    rows = jnp.repeat(jnp.arange(ROWS), GRID_W).astype(jnp.float32)
    cols = jnp.tile(jnp.arange(GRID_W), ROWS).astype(jnp.float32)
    n_freq = HEAD_DIM // 4
    freqs = ROPE_THETA ** (-jnp.arange(n_freq, dtype=jnp.float32) / n_freq)
    ang_row = rows[:, None] * freqs
    ang_col = cols[:, None] * freqs

    for i in range(DEPTH):
        last = i == DEPTH - 1
        sh1, sc1, ga1, sh2, sc2, ga2 = [m[:, None, :] for m in adaln(c, w_mod[i], b_mod[i])]
        csh1, csc1, cga1, csh2, csc2, cga2 = adaln(c_ctx, w_mod[i], b_mod[i])

        w_in_i = w_in[i]
        h_lat = modulate(rms_norm(x, g_pre_mix[i]), sh1, sc1)
        h_ctx = modulate(rms_norm(ctx, g_pre_mix[i]), csh1, csc1)
        q_l, k_l, v_l, pool_l, gl_l = split_projection(h_lat, w_in_i, g_q[i], g_k[i])
        q_c, k_c, v_c, pool_c, gl_c = split_projection(h_ctx, w_in_i, g_q[i], g_k[i])
        gl_l = gl_l + b_gate[i]
        gl_c = gl_c + b_gate[i]
        q_l = axial_rope(q_l, ang_row, ang_col)
        k_l = axial_rope(k_l, ang_row, ang_col)

        attn_l = latent_attention(q_l, k_l, v_l, k_c, v_c)
        pooled_l = multiscale_pool(pool_l, w_pool_grp[i], pool_scale[i])
        y_l = merge_branches(attn_l, pooled_l, gl_l, w_attn_up[i], w_pool_up[i], w_out[i])
        x = x + ga1 * rms_norm(y_l, g_post_mix[i])

        if not last:
            attn_c = context_attention(q_c, k_c, v_c)
            pooled_c = multiscale_pool(pool_c, w_pool_grp[i], pool_scale[i])
            y_c = merge_branches(attn_c, pooled_c, gl_c, w_attn_up[i], w_pool_up[i], w_out[i])
            ctx = ctx + cga1 * rms_norm(y_c, g_post_mix[i])

        h2 = modulate(rms_norm(x, g_pre_mlp[i]), sh2, sc2)
        x = x + ga2 * rms_norm(sq_relu_mlp(h2, w_ff1[i], w_ff2[i]), g_post_mlp[i])
        if not last:
            h2c = modulate(rms_norm(ctx, g_pre_mlp[i]), csh2, csc2)
            ctx = ctx + cga2 * rms_norm(sq_relu_mlp(h2c, w_ff1[i], w_ff2[i]), g_post_mlp[i])

    return x
```

```python
import functools
import math

import jax
import jax.numpy as jnp
from jax import lax
from jax.experimental import pallas as pl
from jax.experimental.pallas import tpu as pltpu

D_MODEL = 1024
GRID_W = 64
HEAD_DIM = 128
N_Q_HEADS = 8
N_KV_HEADS = 2
Q_GROUP = N_Q_HEADS // N_KV_HEADS
ATTN_WIDTH = N_Q_HEADS * HEAD_DIM
KV_WIDTH = N_KV_HEADS * HEAD_DIM
POOL_WINDOWS = (2, 4, 8, 16)
POOL_WIDTH = 512
POOL_GROUP_WIDTH = POOL_WIDTH // len(POOL_WINDOWS)
GATE_WIDTH = 2 * D_MODEL
K_OFF = ATTN_WIDTH
V_OFF = K_OFF + KV_WIDTH
POOL_OFF = V_OFF + KV_WIDTH
GATE_OFF = POOL_OFF + POOL_WIDTH
D_FF = 4 * D_MODEL
ROPE_THETA = 10000.0
EPS = 1e-6
N_MOD = 6

F32_SUBLANES = 8
MXU_WIDTH = 256
VMEM_LIMIT_BYTES = 56 * 1024 * 1024

POOL_HALO = max(POOL_WINDOWS) // 2
assert POOL_HALO == F32_SUBLANES

Q_SCALE = (HEAD_DIM ** -0.5) * math.log2(math.e)

BF16 = jnp.bfloat16
F32 = jnp.float32


def _dot(a, b):
    return jnp.dot(a, b, preferred_element_type=F32)


def _dot_nt(a, b):
    return lax.dot_general(a, b, (((1,), (1,)), ((), ())), preferred_element_type=F32)


def _rms(x, gain):
    ms = jnp.mean(x * x, axis=-1, keepdims=True)
    return x * lax.rsqrt(ms + EPS) * gain


def _norm_mod(x, gain, shift, scale):
    return _rms(x, gain) * (1.0 + scale) + shift


def _params(*sem):
    return pltpu.CompilerParams(dimension_semantics=sem, vmem_limit_bytes=VMEM_LIMIT_BYTES)


def _adaln_kernel(c_ref, w_ref, b_ref, o_ref):
    c = c_ref[...]
    a = (c * jax.nn.sigmoid(c)).astype(BF16)
    o_ref[...] = _dot(a, w_ref[...].astype(BF16)) + b_ref[...]


def _adaln(cond, w_mod, b_mod):
    rows = cond.shape[0]
    n_out = w_mod.shape[1]
    bn = D_MODEL
    return pl.pallas_call(
        _adaln_kernel,
        grid=(n_out // bn,),
        in_specs=[pl.BlockSpec((rows, D_MODEL), lambda j: (0, 0)),
                  pl.BlockSpec((D_MODEL, bn), lambda j: (0, j)),
                  pl.BlockSpec((1, bn), lambda j: (0, j))],
        out_specs=pl.BlockSpec((rows, bn), lambda j: (0, j)),
        out_shape=jax.ShapeDtypeStruct((rows, n_out), F32),
        compiler_params=_params("parallel"),
        name="adaln",
    )(cond, w_mod, b_mod.reshape(1, n_out))


def _ctx_kv_kernel(x_ref, mod_ref, gpre_ref, w_ref, gk_ref, k_ref, v_ref):
    h = _norm_mod(x_ref[0], gpre_ref[...], mod_ref[0, 0:1, :], mod_ref[0, 1:2, :]).astype(BF16)
    pk = _dot(h, w_ref[:, :KV_WIDTH])
    pv = _dot(h, w_ref[:, KV_WIDTH:])
    for e in range(N_KV_HEADS):
        sl = slice(e * HEAD_DIM, (e + 1) * HEAD_DIM)
        k_ref[0, e] = _rms(pk[:, sl], gk_ref[...]).astype(BF16)
        v_ref[0, e] = pv[:, sl].astype(BF16)


def _ctx_kv(ctx, mod_ctx, g_pre, w_kv, g_k):
    b, n, _ = ctx.shape
    kv_shape = jax.ShapeDtypeStruct((b, N_KV_HEADS, n, HEAD_DIM), BF16)
    kv_spec = pl.BlockSpec((1, N_KV_HEADS, n, HEAD_DIM), lambda i: (i, 0, 0, 0))
    return pl.pallas_call(
        _ctx_kv_kernel,
        grid=(b,),
        in_specs=[pl.BlockSpec((1, n, D_MODEL), lambda i: (i, 0, 0)),
                  pl.BlockSpec((1, N_MOD, D_MODEL), lambda i: (0, 0, 0)),
                  pl.BlockSpec((1, D_MODEL), lambda i: (0, 0)),
                  pl.BlockSpec((D_MODEL, 2 * KV_WIDTH), lambda i: (0, 0)),
                  pl.BlockSpec((1, HEAD_DIM), lambda i: (0, 0))],
        out_specs=[kv_spec, kv_spec],
        out_shape=[kv_shape, kv_shape],
        compiler_params=_params("parallel"),
        name="ctx_kv",
    )(ctx, mod_ctx, g_pre, w_kv, g_k)


def _in_proj_kernel(x_ref, xp_ref, xn_ref, mod_ref, gpre_ref, w_ref, bg_ref, gq_ref, gk_ref,
                    cos_ref, sin_ref, wg_ref, ps_ref,
                    q_ref, k_ref, v_ref, pool_ref, gate_ref, *, tm, n_tiles, seq):
    i = pl.program_id(1)
    gpre = gpre_ref[...]
    shift = mod_ref[0, 0:1, :]
    scale = mod_ref[0, 1:2, :]
    hb = _norm_mod(x_ref[0], gpre, shift, scale).astype(BF16)

    cos = cos_ref[...]
    sin = sin_ref[...]
    lane = lax.broadcasted_iota(jnp.int32, (tm, HEAD_DIM), 1)
    low_half = (lane & (HEAD_DIM // 4)) == 0

    def rope(t):
        partner = jnp.where(low_half, pltpu.roll(t, HEAD_DIM - HEAD_DIM // 4, 1),
                            pltpu.roll(t, HEAD_DIM // 4, 1))
        return t * cos + partner * sin

    for j in range(ATTN_WIDTH // MXU_WIDTH):
        pq = _dot(hb, w_ref[:, j * MXU_WIDTH:(j + 1) * MXU_WIDTH])
        for e in range(MXU_WIDTH // HEAD_DIM):
            t = pq[:, e * HEAD_DIM:(e + 1) * HEAD_DIM]
            t = rope(_rms(t, gq_ref[...])) * Q_SCALE
            q_ref[0, j * (MXU_WIDTH // HEAD_DIM) + e] = t.astype(BF16)

    pk = _dot(hb, w_ref[:, K_OFF:K_OFF + KV_WIDTH])
    pv = _dot(hb, w_ref[:, V_OFF:V_OFF + KV_WIDTH])
    for e in range(N_KV_HEADS):
        sl = slice(e * HEAD_DIM, (e + 1) * HEAD_DIM)
        k_ref[0, e] = rope(_rms(pk[:, sl], gk_ref[...])).astype(BF16)
        v_ref[0, e] = pv[:, sl].astype(BF16)

    w_pool = w_ref[:, POOL_OFF:POOL_OFF + POOL_WIDTH]
    u_main = _dot(hb, w_pool)
    x_halo = jnp.concatenate([xp_ref[0], xn_ref[0]], axis=0)
    u_halo = _dot(_norm_mod(x_halo, gpre, shift, scale).astype(BF16), w_pool)
    u_prev = jnp.where(i > 0, u_halo[:POOL_HALO], 0.0)
    u_next = jnp.where(i < n_tiles - 1, u_halo[POOL_HALO:], 0.0)
    u_ext = jnp.concatenate([u_prev, u_main, u_next], axis=0)
    n_ext = tm + 2 * POOL_HALO
    t_glob = i * tm + lax.broadcasted_iota(jnp.int32, (tm, POOL_GROUP_WIDTH), 0)
    mixed = []
    for gi, win in enumerate(POOL_WINDOWS):
        sl = slice(gi * POOL_GROUP_WIDTH, (gi + 1) * POOL_GROUP_WIDTH)
        half = win // 2
        run = u_ext[:, sl]
        length = 1
        while length < half:
            run = run + pltpu.roll(run, n_ext - length, 0)
            length *= 2
        window = run + pltpu.roll(run, half, 0)
        window = window[POOL_HALO:POOL_HALO + tm]
        cnt = (jnp.minimum(t_glob + half, seq) - jnp.maximum(t_glob - half, 0)).astype(F32)
        pooled = window / cnt - u_main[:, sl]
        mixed.append(_dot(pooled.astype(BF16), wg_ref[gi]))
    pool_ref[0] = (jnp.concatenate(mixed, axis=-1) * ps_ref[...]).astype(BF16)

    gate_chunk = 2 * MXU_WIDTH
    for j in range(GATE_WIDTH // gate_chunk):
        sl = slice(j * gate_chunk, (j + 1) * gate_chunk)
        logits = _dot(hb, w_ref[:, GATE_OFF + j * gate_chunk:GATE_OFF + (j + 1) * gate_chunk])
        gate_ref[0, :, sl] = jax.nn.sigmoid(logits + bg_ref[:, sl]).astype(BF16)


def _in_proj(x, mod, g_pre, w_in, b_gate, g_q, g_k, cos_t, sin_t, w_grp, pool_scale, *, tm):
    b, s, _ = x.shape
    n_tiles = s // tm
    halo_blocks_per_tile = tm // POOL_HALO
    last_halo_block = s // POOL_HALO - 1
    const2 = lambda bi, i: (0, 0)
    in_specs = [
        pl.BlockSpec((1, tm, D_MODEL), lambda bi, i: (bi, i, 0)),
        pl.BlockSpec((1, POOL_HALO, D_MODEL),
                     lambda bi, i: (bi, jnp.maximum(i * halo_blocks_per_tile - 1, 0), 0)),
        pl.BlockSpec((1, POOL_HALO, D_MODEL),
                     lambda bi, i: (bi, jnp.minimum((i + 1) * halo_blocks_per_tile, last_halo_block), 0)),
        pl.BlockSpec((1, N_MOD, D_MODEL), lambda bi, i: (bi, 0, 0)),
        pl.BlockSpec((1, D_MODEL), const2),
        pl.BlockSpec(w_in.shape, const2),
        pl.BlockSpec((1, GATE_WIDTH), const2),
        pl.BlockSpec((1, HEAD_DIM), const2),
        pl.BlockSpec((1, HEAD_DIM), const2),
        pl.BlockSpec((tm, HEAD_DIM), lambda bi, i: (i, 0)),
        pl.BlockSpec((tm, HEAD_DIM), lambda bi, i: (i, 0)),
        pl.BlockSpec(w_grp.shape, lambda bi, i: (0, 0, 0)),
        pl.BlockSpec((1, POOL_WIDTH), const2),
    ]
    out_specs = [
        pl.BlockSpec((1, N_Q_HEADS, tm, HEAD_DIM), lambda bi, i: (bi, 0, i, 0)),
        pl.BlockSpec((1, N_KV_HEADS, tm, HEAD_DIM), lambda bi, i: (bi, 0, i, 0)),
        pl.BlockSpec((1, N_KV_HEADS, tm, HEAD_DIM), lambda bi, i: (bi, 0, i, 0)),
        pl.BlockSpec((1, tm, POOL_WIDTH), lambda bi, i: (bi, i, 0)),
        pl.BlockSpec((1, tm, GATE_WIDTH), lambda bi, i: (bi, i, 0)),
    ]
    out_shape = [
        jax.ShapeDtypeStruct((b, N_Q_HEADS, s, HEAD_DIM), BF16),
        jax.ShapeDtypeStruct((b, N_KV_HEADS, s, HEAD_DIM), BF16),
        jax.ShapeDtypeStruct((b, N_KV_HEADS, s, HEAD_DIM), BF16),
        jax.ShapeDtypeStruct((b, s, POOL_WIDTH), BF16),
        jax.ShapeDtypeStruct((b, s, GATE_WIDTH), BF16),
    ]
    return pl.pallas_call(
        functools.partial(_in_proj_kernel, tm=tm, n_tiles=n_tiles, seq=s),
        grid=(b, n_tiles),
        in_specs=in_specs,
        out_specs=out_specs,
        out_shape=out_shape,
        compiler_params=_params("parallel", "parallel"),
        name="in_proj",
    )(x, x, x, mod, g_pre, w_in, b_gate, g_q, g_k, cos_t, sin_t, w_grp, pool_scale)


def _attention_kernel(q_ref, kc_ref, vc_ref, kl_ref, vl_ref, o_ref, *, tq, rows):
    kc = kc_ref[0, 0]
    vc = vc_ref[0, 0]
    kl = kl_ref[0, 0]
    vl = vl_ref[0, 0]
    for h in range(Q_GROUP):
        def body(r, carry):
            rs = pl.ds(pl.multiple_of(r * rows, rows), rows)
            q = q_ref[0, 0, h, rs, :]
            s_c = _dot_nt(q, kc)
            s_l = _dot_nt(q, kl)
            m = jnp.maximum(jnp.max(s_c, axis=-1, keepdims=True),
                            jnp.max(s_l, axis=-1, keepdims=True))
            p_c = jnp.exp2(s_c - m)
            p_l = jnp.exp2(s_l - m)
            denom = jnp.sum(p_c, axis=-1, keepdims=True) + jnp.sum(p_l, axis=-1, keepdims=True)
            o = _dot(p_c.astype(BF16), vc) + _dot(p_l.astype(BF16), vl)
            o_ref[0, rs, h * HEAD_DIM:(h + 1) * HEAD_DIM] = (o / denom).astype(BF16)
            return carry
        lax.fori_loop(0, tq // rows, body, 0)


def _attention(q, k_ctx, v_ctx, k_lat, v_lat, *, tq, rows):
    b, _, s, _ = q.shape
    n_ctx = k_ctx.shape[2]
    qg = q.reshape(b, N_KV_HEADS, Q_GROUP, s, HEAD_DIM)
    kv_idx = lambda bi, kh, i: (bi, kh, 0, 0)
    return pl.pallas_call(
        functools.partial(_attention_kernel, tq=tq, rows=rows),
        grid=(b, N_KV_HEADS, s // tq),
        in_specs=[pl.BlockSpec((1, 1, Q_GROUP, tq, HEAD_DIM), lambda bi, kh, i: (bi, kh, 0, i, 0)),
                  pl.BlockSpec((1, 1, n_ctx, HEAD_DIM), kv_idx),
                  pl.BlockSpec((1, 1, n_ctx, HEAD_DIM), kv_idx),
                  pl.BlockSpec((1, 1, s, HEAD_DIM), kv_idx),
                  pl.BlockSpec((1, 1, s, HEAD_DIM), kv_idx)],
        out_specs=pl.BlockSpec((1, tq, Q_GROUP * HEAD_DIM), lambda bi, kh, i: (bi, i, kh)),
        out_shape=jax.ShapeDtypeStruct((b, s, ATTN_WIDTH), BF16),
        compiler_params=_params("parallel", "parallel", "parallel"),
        name="attention",
    )(qg, k_ctx, v_ctx, k_lat, v_lat)


def _merge_kernel(x_ref, a_ref, p_ref, g_ref, mod_ref, gpost_ref, wa_ref, wp_ref, wo_ref, o_ref):
    ya = _dot(a_ref[0], wa_ref[...])
    yp = _dot(p_ref[0], wp_ref[...])
    ga = g_ref[0, :, :D_MODEL].astype(F32)
    gp = g_ref[0, :, D_MODEL:].astype(F32)
    y = _dot((ga * ya + gp * yp).astype(BF16), wo_ref[...])
    o_ref[0] = x_ref[0] + mod_ref[0, 2:3, :] * _rms(y, gpost_ref[...])


def _merge(x, attn_o, pool_o, gates, mod, g_post, w_attn_up, w_pool_up, w_out, *, tm):
    b, s, _ = x.shape
    tile = lambda width: pl.BlockSpec((1, tm, width), lambda bi, i: (bi, i, 0))
    const2 = lambda bi, i: (0, 0)
    return pl.pallas_call(
        _merge_kernel,
        grid=(b, s // tm),
        in_specs=[tile(D_MODEL), tile(ATTN_WIDTH), tile(POOL_WIDTH), tile(GATE_WIDTH),
                  pl.BlockSpec((1, N_MOD, D_MODEL), lambda bi, i: (bi, 0, 0)),
                  pl.BlockSpec((1, D_MODEL), const2),
                  pl.BlockSpec(w_attn_up.shape, const2),
                  pl.BlockSpec(w_pool_up.shape, const2),
                  pl.BlockSpec(w_out.shape, const2)],
        out_specs=tile(D_MODEL),
        out_shape=jax.ShapeDtypeStruct(x.shape, F32),
        compiler_params=_params("parallel", "parallel"),
        name="merge",
    )(x, attn_o, pool_o, gates, mod, g_post, w_attn_up, w_pool_up, w_out)


def _mlp_kernel(x_ref, mod_ref, gpre_ref, gpost_ref, w1_ref, w2_ref, o_ref, *, ff_chunk):
    x = x_ref[0]
    h = _norm_mod(x, gpre_ref[...], mod_ref[0, 3:4, :], mod_ref[0, 4:5, :]).astype(BF16)
    acc = None
    for j in range(D_FF // ff_chunk):
        sl = slice(j * ff_chunk, (j + 1) * ff_chunk)
        t = jnp.maximum(_dot(h, w1_ref[:, sl]), 0.0)
        part = _dot((t * t).astype(BF16), w2_ref[sl, :])
        acc = part if acc is None else acc + part
    o_ref[0] = x + mod_ref[0, 5:6, :] * _rms(acc, gpost_ref[...])


def _mlp(x, mod, g_pre, g_post, w_ff1, w_ff2, *, tm, ff_chunk):
    b, s, _ = x.shape
    tile = pl.BlockSpec((1, tm, D_MODEL), lambda bi, i: (bi, i, 0))
    const2 = lambda bi, i: (0, 0)
    return pl.pallas_call(
        functools.partial(_mlp_kernel, ff_chunk=ff_chunk),
        grid=(b, s // tm),
        in_specs=[tile,
                  pl.BlockSpec((1, N_MOD, D_MODEL), lambda bi, i: (bi, 0, 0)),
                  pl.BlockSpec((1, D_MODEL), const2),
                  pl.BlockSpec((1, D_MODEL), const2),
                  pl.BlockSpec(w_ff1.shape, const2),
                  pl.BlockSpec(w_ff2.shape, const2)],
        out_specs=tile,
        out_shape=jax.ShapeDtypeStruct(x.shape, F32),
        compiler_params=_params("parallel", "parallel"),
        name="mlp",
    )(x, mod, g_pre, g_post, w_ff1, w_ff2)


def _rope_tables(seq):
    t = jnp.arange(seq)
    rows = (t // GRID_W).astype(F32)
    cols = (t % GRID_W).astype(F32)
    n_freq = HEAD_DIM // 4
    freqs = ROPE_THETA ** (-jnp.arange(n_freq, dtype=F32) / n_freq)
    ang_row = rows[:, None] * freqs
    ang_col = cols[:, None] * freqs
    cos_r, sin_r, cos_c, sin_c = jnp.cos(ang_row), jnp.sin(ang_row), jnp.cos(ang_col), jnp.sin(ang_col)
    cos_t = jnp.concatenate([cos_r, cos_r, cos_c, cos_c], axis=-1)
    sin_t = jnp.concatenate([-sin_r, sin_r, -sin_c, sin_c], axis=-1)
    return cos_t, sin_t


def kernel(x, c, ctx, c_ctx, w_mod, b_mod, g_pre_mix, g_post_mix, g_pre_mlp, g_post_mlp, w_in, b_gate, g_q, g_k, w_attn_up, w_pool_grp, pool_scale, w_pool_up, w_out, w_ff1, w_ff2):
    depth = w_mod.shape[0]
    assert depth == 1, "single-layer block"
    b, s, _ = x.shape
    row = lambda a: a[0].reshape(1, -1)

    n_cond = -(-(b + 1) // F32_SUBLANES) * F32_SUBLANES
    cond = jnp.concatenate([c, c_ctx[None, :], jnp.zeros((n_cond - b - 1, D_MODEL), F32)], axis=0)
    mod_all = _adaln(cond, w_mod[0], b_mod[0])
    mod = mod_all[:b].reshape(b, N_MOD, D_MODEL)
    mod_ctx = mod_all[b:b + 1].reshape(1, N_MOD, D_MODEL)

    w_in_b = w_in[0].astype(BF16)
    cos_t, sin_t = _rope_tables(s)

    k_ctx, v_ctx = _ctx_kv(ctx, mod_ctx, row(g_pre_mix), w_in_b[:, K_OFF:POOL_OFF], row(g_k))
    q, k_lat, v_lat, pool_o, gates = _in_proj(
        x, mod, row(g_pre_mix), w_in_b, row(b_gate), row(g_q), row(g_k), cos_t, sin_t,
        w_pool_grp[0].astype(BF16), row(pool_scale), tm=512)
    attn_o = _attention(q, k_ctx, v_ctx, k_lat, v_lat, tq=512, rows=256)
    x1 = _merge(x, attn_o, pool_o, gates, mod, row(g_post_mix), w_attn_up[0].astype(BF16),
                w_pool_up[0].astype(BF16), w_out[0].astype(BF16), tm=512)
    return _mlp(x1, mod, row(g_pre_mlp), row(g_post_mlp), w_ff1[0].astype(BF16),
                w_ff2[0].astype(BF16), tm=512, ff_chunk=1024)
```

```python
import functools
import math

import jax
import jax.numpy as jnp
from jax import lax
from jax.experimental import pallas as pl
from jax.experimental.pallas import tpu as pltpu

D_MODEL = 1024
GRID_W = 64
HEAD_DIM = 128
N_Q_HEADS = 8
N_KV_HEADS = 2
Q_GROUP = N_Q_HEADS // N_KV_HEADS
ATTN_WIDTH = N_Q_HEADS * HEAD_DIM
KV_WIDTH = N_KV_HEADS * HEAD_DIM
POOL_WINDOWS = (2, 4, 8, 16)
POOL_WIDTH = 512
POOL_GROUP_WIDTH = POOL_WIDTH // len(POOL_WINDOWS)
GATE_WIDTH = 2 * D_MODEL
K_OFF = ATTN_WIDTH
V_OFF = K_OFF + KV_WIDTH
POOL_OFF = V_OFF + KV_WIDTH
GATE_OFF = POOL_OFF + POOL_WIDTH
D_FF = 4 * D_MODEL
ROPE_THETA = 10000.0
EPS = 1e-6
N_MOD = 6

F32_SUBLANES = 8
MXU_WIDTH = 256
VMEM_LIMIT_BYTES = 56 * 1024 * 1024

POOL_HALO = max(POOL_WINDOWS) // 2
assert POOL_HALO == F32_SUBLANES

Q_SCALE = (HEAD_DIM ** -0.5) * math.log2(math.e)

BF16 = jnp.bfloat16
F32 = jnp.float32


def _dot(a, b):
    return jnp.dot(a, b, preferred_element_type=F32)


def _dot_nt(a, b):
    return lax.dot_general(a, b, (((1,), (1,)), ((), ())), preferred_element_type=F32)


def _rms(x, gain):
    ms = jnp.mean(x * x, axis=-1, keepdims=True)
    return x * lax.rsqrt(ms + EPS) * gain


def _norm_mod(x, gain, shift, scale):
    return _rms(x, gain) * (1.0 + scale) + shift


def _params(*sem):
    return pltpu.CompilerParams(dimension_semantics=sem, vmem_limit_bytes=VMEM_LIMIT_BYTES)


def _adaln_kernel(c_ref, w_ref, b_ref, o_ref):
    c = c_ref[...]
    a = (c * jax.nn.sigmoid(c)).astype(BF16)
    o_ref[...] = _dot(a, w_ref[...].astype(BF16)) + b_ref[...]


def _adaln(cond, w_mod, b_mod):
    rows = cond.shape[0]
    n_out = w_mod.shape[1]
    bn = D_MODEL
    return pl.pallas_call(
        _adaln_kernel,
        grid=(n_out // bn,),
        in_specs=[pl.BlockSpec((rows, D_MODEL), lambda j: (0, 0)),
                  pl.BlockSpec((D_MODEL, bn), lambda j: (0, j)),
                  pl.BlockSpec((1, bn), lambda j: (0, j))],
        out_specs=pl.BlockSpec((rows, bn), lambda j: (0, j)),
        out_shape=jax.ShapeDtypeStruct((rows, n_out), F32),
        compiler_params=_params("parallel"),
        name="adaln",
    )(cond, w_mod, b_mod.reshape(1, n_out))


def _ctx_kv_kernel(x_ref, mod_ref, gpre_ref, w_ref, gk_ref, k_ref, v_ref):
    h = _norm_mod(x_ref[0], gpre_ref[...], mod_ref[0, 0:1, :], mod_ref[0, 1:2, :]).astype(BF16)
    pk = _dot(h, w_ref[:, :KV_WIDTH])
    pv = _dot(h, w_ref[:, KV_WIDTH:])
    for e in range(N_KV_HEADS):
        sl = slice(e * HEAD_DIM, (e + 1) * HEAD_DIM)
        k_ref[0, e] = _rms(pk[:, sl], gk_ref[...]).astype(BF16)
        v_ref[0, e] = pv[:, sl].astype(BF16)


def _ctx_kv(ctx, mod_ctx, g_pre, w_kv, g_k):
    b, n, _ = ctx.shape
    kv_shape = jax.ShapeDtypeStruct((b, N_KV_HEADS, n, HEAD_DIM), BF16)
    kv_spec = pl.BlockSpec((1, N_KV_HEADS, n, HEAD_DIM), lambda i: (i, 0, 0, 0))
    return pl.pallas_call(
        _ctx_kv_kernel,
        grid=(b,),
        in_specs=[pl.BlockSpec((1, n, D_MODEL), lambda i: (i, 0, 0)),
                  pl.BlockSpec((1, N_MOD, D_MODEL), lambda i: (0, 0, 0)),
                  pl.BlockSpec((1, D_MODEL), lambda i: (0, 0)),
                  pl.BlockSpec((D_MODEL, 2 * KV_WIDTH), lambda i: (0, 0)),
                  pl.BlockSpec((1, HEAD_DIM), lambda i: (0, 0))],
        out_specs=[kv_spec, kv_spec],
        out_shape=[kv_shape, kv_shape],
        compiler_params=_params("parallel"),
        name="ctx_kv",
    )(ctx, mod_ctx, g_pre, w_kv, g_k)


def _in_proj_kernel(x_ref, xp_ref, xn_ref, mod_ref, gpre_ref, w_ref, bg_ref, gq_ref, gk_ref,
                    cos_ref, sin_ref, wg_ref, ps_ref,
                    q_ref, k_ref, v_ref, pool_ref, gate_ref, *, tm, n_tiles, seq):
    i = pl.program_id(1)
    gpre = gpre_ref[...]
    shift = mod_ref[0, 0:1, :]
    scale = mod_ref[0, 1:2, :]
    hb = _norm_mod(x_ref[0], gpre, shift, scale).astype(BF16)

    cos = cos_ref[...]
    sin = sin_ref[...]
    lane = lax.broadcasted_iota(jnp.int32, (tm, HEAD_DIM), 1)
    low_half = (lane & (HEAD_DIM // 4)) == 0

    def rope(t):
        partner = jnp.where(low_half, pltpu.roll(t, HEAD_DIM - HEAD_DIM // 4, 1),
                            pltpu.roll(t, HEAD_DIM // 4, 1))
        return t * cos + partner * sin

    for j in range(ATTN_WIDTH // MXU_WIDTH):
        pq = _dot(hb, w_ref[:, j * MXU_WIDTH:(j + 1) * MXU_WIDTH])
        for e in range(MXU_WIDTH // HEAD_DIM):
            t = pq[:, e * HEAD_DIM:(e + 1) * HEAD_DIM]
            t = rope(_rms(t, gq_ref[...])) * Q_SCALE
            q_ref[0, j * (MXU_WIDTH // HEAD_DIM) + e] = t.astype(BF16)

    pk = _dot(hb, w_ref[:, K_OFF:K_OFF + KV_WIDTH])
    pv = _dot(hb, w_ref[:, V_OFF:V_OFF + KV_WIDTH])
    for e in range(N_KV_HEADS):
        sl = slice(e * HEAD_DIM, (e + 1) * HEAD_DIM)
        k_ref[0, e] = rope(_rms(pk[:, sl], gk_ref[...])).astype(BF16)
        v_ref[0, e] = pv[:, sl].astype(BF16)

    w_pool = w_ref[:, POOL_OFF:POOL_OFF + POOL_WIDTH]
    u_main = _dot(hb, w_pool)
    x_halo = jnp.concatenate([xp_ref[0], xn_ref[0]], axis=0)
    u_halo = _dot(_norm_mod(x_halo, gpre, shift, scale).astype(BF16), w_pool)
    u_prev = jnp.where(i > 0, u_halo[:POOL_HALO], 0.0)
    u_next = jnp.where(i < n_tiles - 1, u_halo[POOL_HALO:], 0.0)
    u_ext = jnp.concatenate([u_prev, u_main, u_next], axis=0)
    n_ext = tm + 2 * POOL_HALO
    t_glob = i * tm + lax.broadcasted_iota(jnp.int32, (tm, POOL_GROUP_WIDTH), 0)
    mixed = []
    for gi, win in enumerate(POOL_WINDOWS):
        sl = slice(gi * POOL_GROUP_WIDTH, (gi + 1) * POOL_GROUP_WIDTH)
        half = win // 2
        run = u_ext[:, sl]
        length = 1
        while length < half:
            run = run + pltpu.roll(run, n_ext - length, 0)
            length *= 2
        window = run + pltpu.roll(run, half, 0)
        window = window[POOL_HALO:POOL_HALO + tm]
        cnt = (jnp.minimum(t_glob + half, seq) - jnp.maximum(t_glob - half, 0)).astype(F32)
        pooled = window / cnt - u_main[:, sl]
        mixed.append(_dot(pooled.astype(BF16), wg_ref[gi]))
    pool_ref[0] = (jnp.concatenate(mixed, axis=-1) * ps_ref[...]).astype(BF16)

    gate_chunk = 2 * MXU_WIDTH
    for j in range(GATE_WIDTH // gate_chunk):
        sl = slice(j * gate_chunk, (j + 1) * gate_chunk)
        logits = _dot(hb, w_ref[:, GATE_OFF + j * gate_chunk:GATE_OFF + (j + 1) * gate_chunk])
        gate_ref[0, :, sl] = jax.nn.sigmoid(logits + bg_ref[:, sl]).astype(BF16)


def _in_proj(x, mod, g_pre, w_in, b_gate, g_q, g_k, cos_t, sin_t, w_grp, pool_scale, *, tm):
    b, s, _ = x.shape
    n_tiles = s // tm
    halo_blocks_per_tile = tm // POOL_HALO
    last_halo_block = s // POOL_HALO - 1
    const2 = lambda bi, i: (0, 0)
    in_specs = [
        pl.BlockSpec((1, tm, D_MODEL), lambda bi, i: (bi, i, 0)),
        pl.BlockSpec((1, POOL_HALO, D_MODEL),
                     lambda bi, i: (bi, jnp.maximum(i * halo_blocks_per_tile - 1, 0), 0)),
        pl.BlockSpec((1, POOL_HALO, D_MODEL),
                     lambda bi, i: (bi, jnp.minimum((i + 1) * halo_blocks_per_tile, last_halo_block), 0)),
        pl.BlockSpec((1, N_MOD, D_MODEL), lambda bi, i: (bi, 0, 0)),
        pl.BlockSpec((1, D_MODEL), const2),
        pl.BlockSpec(w_in.shape, const2),
        pl.BlockSpec((1, GATE_WIDTH), const2),
        pl.BlockSpec((1, HEAD_DIM), const2),
        pl.BlockSpec((1, HEAD_DIM), const2),
        pl.BlockSpec((tm, HEAD_DIM), lambda bi, i: (i, 0)),
        pl.BlockSpec((tm, HEAD_DIM), lambda bi, i: (i, 0)),
        pl.BlockSpec(w_grp.shape, lambda bi, i: (0, 0, 0)),
        pl.BlockSpec((1, POOL_WIDTH), const2),
    ]
    out_specs = [
        pl.BlockSpec((1, N_Q_HEADS, tm, HEAD_DIM), lambda bi, i: (bi, 0, i, 0)),
        pl.BlockSpec((1, N_KV_HEADS, tm, HEAD_DIM), lambda bi, i: (bi, 0, i, 0)),
        pl.BlockSpec((1, N_KV_HEADS, tm, HEAD_DIM), lambda bi, i: (bi, 0, i, 0)),
        pl.BlockSpec((1, tm, POOL_WIDTH), lambda bi, i: (bi, i, 0)),
        pl.BlockSpec((1, tm, GATE_WIDTH), lambda bi, i: (bi, i, 0)),
    ]
    out_shape = [
        jax.ShapeDtypeStruct((b, N_Q_HEADS, s, HEAD_DIM), BF16),
        jax.ShapeDtypeStruct((b, N_KV_HEADS, s, HEAD_DIM), BF16),
        jax.ShapeDtypeStruct((b, N_KV_HEADS, s, HEAD_DIM), BF16),
        jax.ShapeDtypeStruct((b, s, POOL_WIDTH), BF16),
        jax.ShapeDtypeStruct((b, s, GATE_WIDTH), BF16),
    ]
    return pl.pallas_call(
        functools.partial(_in_proj_kernel, tm=tm, n_tiles=n_tiles, seq=s),
        grid=(b, n_tiles),
        in_specs=in_specs,
        out_specs=out_specs,
        out_shape=out_shape,
        compiler_params=_params("parallel", "parallel"),
        name="in_proj",
    )(x, x, x, mod, g_pre, w_in, b_gate, g_q, g_k, cos_t, sin_t, w_grp, pool_scale)


def _attention_kernel(q_ref, kc_ref, vc_ref, kl_ref, vl_ref, o_ref, *, tq, rows):
    kc = kc_ref[0, 0]
    kl = kl_ref[0, 0]
    ones_cols = lambda n: jnp.ones((n, MXU_WIDTH - HEAD_DIM), BF16)
    vc = jnp.concatenate([vc_ref[0, 0], ones_cols(kc.shape[0])], axis=-1)
    vl = jnp.concatenate([vl_ref[0, 0], ones_cols(kl.shape[0])], axis=-1)
    for h in range(Q_GROUP):
        for r in range(tq // rows):
            rs = slice(r * rows, (r + 1) * rows)
            q = q_ref[0, 0, h, rs, :]
            s_c = _dot_nt(q, kc)
            s_l = _dot_nt(q, kl)
            m = jnp.maximum(jnp.max(s_c, axis=-1, keepdims=True),
                            jnp.max(s_l, axis=-1, keepdims=True))
            p_c = jnp.exp2(s_c - m).astype(BF16)
            p_l = jnp.exp2(s_l - m).astype(BF16)
            o = _dot(p_c, vc) + _dot(p_l, vl)
            o_ref[0, rs, h * HEAD_DIM:(h + 1) * HEAD_DIM] = (
                o[:, :HEAD_DIM] / o[:, HEAD_DIM:]).astype(BF16)


def _attention(q, k_ctx, v_ctx, k_lat, v_lat, *, tq, rows):
    b, _, s, _ = q.shape
    n_ctx = k_ctx.shape[2]
    qg = q.reshape(b, N_KV_HEADS, Q_GROUP, s, HEAD_DIM)
    kv_idx = lambda bi, kh, i: (bi, kh, 0, 0)
    return pl.pallas_call(
        functools.partial(_attention_kernel, tq=tq, rows=rows),
        grid=(b, N_KV_HEADS, s // tq),
        in_specs=[pl.BlockSpec((1, 1, Q_GROUP, tq, HEAD_DIM), lambda bi, kh, i: (bi, kh, 0, i, 0)),
                  pl.BlockSpec((1, 1, n_ctx, HEAD_DIM), kv_idx),
                  pl.BlockSpec((1, 1, n_ctx, HEAD_DIM), kv_idx),
                  pl.BlockSpec((1, 1, s, HEAD_DIM), kv_idx),
                  pl.BlockSpec((1, 1, s, HEAD_DIM), kv_idx)],
        out_specs=pl.BlockSpec((1, tq, Q_GROUP * HEAD_DIM), lambda bi, kh, i: (bi, i, kh)),
        out_shape=jax.ShapeDtypeStruct((b, s, ATTN_WIDTH), BF16),
        compiler_params=_params("parallel", "parallel", "parallel"),
        name="attention",
    )(qg, k_ctx, v_ctx, k_lat, v_lat)


def _merge_kernel(x_ref, a_ref, p_ref, g_ref, mod_ref, gpost_ref, wa_ref, wp_ref, wo_ref, o_ref):
    ya = _dot(a_ref[0], wa_ref[...])
    yp = _dot(p_ref[0], wp_ref[...])
    ga = g_ref[0, :, :D_MODEL].astype(F32)
    gp = g_ref[0, :, D_MODEL:].astype(F32)
    y = _dot((ga * ya + gp * yp).astype(BF16), wo_ref[...])
    o_ref[0] = x_ref[0] + mod_ref[0, 2:3, :] * _rms(y, gpost_ref[...])


def _merge(x, attn_o, pool_o, gates, mod, g_post, w_attn_up, w_pool_up, w_out, *, tm):
    b, s, _ = x.shape
    tile = lambda width: pl.BlockSpec((1, tm, width), lambda bi, i: (bi, i, 0))
    const2 = lambda bi, i: (0, 0)
    return pl.pallas_call(
        _merge_kernel,
        grid=(b, s // tm),
        in_specs=[tile(D_MODEL), tile(ATTN_WIDTH), tile(POOL_WIDTH), tile(GATE_WIDTH),
                  pl.BlockSpec((1, N_MOD, D_MODEL), lambda bi, i: (bi, 0, 0)),
                  pl.BlockSpec((1, D_MODEL), const2),
                  pl.BlockSpec(w_attn_up.shape, const2),
                  pl.BlockSpec(w_pool_up.shape, const2),
                  pl.BlockSpec(w_out.shape, const2)],
        out_specs=tile(D_MODEL),
        out_shape=jax.ShapeDtypeStruct(x.shape, F32),
        compiler_params=_params("parallel", "parallel"),
        name="merge",
    )(x, attn_o, pool_o, gates, mod, g_post, w_attn_up, w_pool_up, w_out)


def _mlp_kernel(x_ref, mod_ref, gpre_ref, gpost_ref, w1_ref, w2_ref, o_ref, *, ff_chunk):
    x = x_ref[0]
    h = _norm_mod(x, gpre_ref[...], mod_ref[0, 3:4, :], mod_ref[0, 4:5, :]).astype(BF16)
    acc = None
    for j in range(D_FF // ff_chunk):
        sl = slice(j * ff_chunk, (j + 1) * ff_chunk)
        t = jnp.maximum(_dot(h, w1_ref[:, sl]), 0.0)
        part = _dot((t * t).astype(BF16), w2_ref[sl, :])
        acc = part if acc is None else acc + part
    o_ref[0] = x + mod_ref[0, 5:6, :] * _rms(acc, gpost_ref[...])


def _mlp(x, mod, g_pre, g_post, w_ff1, w_ff2, *, tm, ff_chunk):
    b, s, _ = x.shape
    tile = pl.BlockSpec((1, tm, D_MODEL), lambda bi, i: (bi, i, 0))
    const2 = lambda bi, i: (0, 0)
    return pl.pallas_call(
        functools.partial(_mlp_kernel, ff_chunk=ff_chunk),
        grid=(b, s // tm),
        in_specs=[tile,
                  pl.BlockSpec((1, N_MOD, D_MODEL), lambda bi, i: (bi, 0, 0)),
                  pl.BlockSpec((1, D_MODEL), const2),
                  pl.BlockSpec((1, D_MODEL), const2),
                  pl.BlockSpec(w_ff1.shape, const2),
                  pl.BlockSpec(w_ff2.shape, const2)],
        out_specs=tile,
        out_shape=jax.ShapeDtypeStruct(x.shape, F32),
        compiler_params=_params("parallel", "parallel"),
        name="mlp",
    )(x, mod, g_pre, g_post, w_ff1, w_ff2)


def _rope_tables(seq):
    t = jnp.arange(seq)
    rows = (t // GRID_W).astype(F32)
    cols = (t % GRID_W).astype(F32)
    n_freq = HEAD_DIM // 4
    freqs = ROPE_THETA ** (-jnp.arange(n_freq, dtype=F32) / n_freq)
    ang_row = rows[:, None] * freqs
    ang_col = cols[:, None] * freqs
    cos_r, sin_r, cos_c, sin_c = jnp.cos(ang_row), jnp.sin(ang_row), jnp.cos(ang_col), jnp.sin(ang_col)
    cos_t = jnp.concatenate([cos_r, cos_r, cos_c, cos_c], axis=-1)
    sin_t = jnp.concatenate([-sin_r, sin_r, -sin_c, sin_c], axis=-1)
    return cos_t, sin_t


def kernel(x, c, ctx, c_ctx, w_mod, b_mod, g_pre_mix, g_post_mix, g_pre_mlp, g_post_mlp, w_in, b_gate, g_q, g_k, w_attn_up, w_pool_grp, pool_scale, w_pool_up, w_out, w_ff1, w_ff2):
    depth = w_mod.shape[0]
    assert depth == 1, "single-layer block"
    b, s, _ = x.shape
    row = lambda a: a[0].reshape(1, -1)

    n_cond = -(-(b + 1) // F32_SUBLANES) * F32_SUBLANES
    cond = jnp.concatenate([c, c_ctx[None, :], jnp.zeros((n_cond - b - 1, D_MODEL), F32)], axis=0)
    mod_all = _adaln(cond, w_mod[0], b_mod[0])
    mod = mod_all[:b].reshape(b, N_MOD, D_MODEL)
    mod_ctx = mod_all[b:b + 1].reshape(1, N_MOD, D_MODEL)

    w_in_b = w_in[0].astype(BF16)
    cos_t, sin_t = _rope_tables(s)

    k_ctx, v_ctx = _ctx_kv(ctx, mod_ctx, row(g_pre_mix), w_in_b[:, K_OFF:POOL_OFF], row(g_k))
    q, k_lat, v_lat, pool_o, gates = _in_proj(
        x, mod, row(g_pre_mix), w_in_b, row(b_gate), row(g_q), row(g_k), cos_t, sin_t,
        w_pool_grp[0].astype(BF16), row(pool_scale), tm=512)
    attn_o = _attention(q, k_ctx, v_ctx, k_lat, v_lat, tq=1024, rows=128)
    x1 = _merge(x, attn_o, pool_o, gates, mod, row(g_post_mix), w_attn_up[0].astype(BF16),
                w_pool_up[0].astype(BF16), w_out[0].astype(BF16), tm=512)
    return _mlp(x1, mod, row(g_pre_mlp), row(g_post_mlp), w_ff1[0].astype(BF16),
                w_ff2[0].astype(BF16), tm=512, ff_chunk=1024)
```

```python
import functools
import math

import jax
import jax.numpy as jnp
from jax import lax
from jax.experimental import pallas as pl
from jax.experimental.pallas import tpu as pltpu

D_MODEL = 1024
GRID_W = 64
HEAD_DIM = 128
N_Q_HEADS = 8
N_KV_HEADS = 2
Q_GROUP = N_Q_HEADS // N_KV_HEADS
ATTN_WIDTH = N_Q_HEADS * HEAD_DIM
KV_WIDTH = N_KV_HEADS * HEAD_DIM
POOL_WINDOWS = (2, 4, 8, 16)
POOL_WIDTH = 512
POOL_GROUP_WIDTH = POOL_WIDTH // len(POOL_WINDOWS)
GATE_WIDTH = 2 * D_MODEL
K_OFF = ATTN_WIDTH
V_OFF = K_OFF + KV_WIDTH
POOL_OFF = V_OFF + KV_WIDTH
GATE_OFF = POOL_OFF + POOL_WIDTH
D_FF = 4 * D_MODEL
ROPE_THETA = 10000.0
ROPE_PARTNER = HEAD_DIM // 4
EPS = 1e-6
N_MOD = 6

F32_SUBLANES = 8
MXU_WIDTH = 256
VMEM_LIMIT_BYTES = 56 * 1024 * 1024

POOL_HALO = max(POOL_WINDOWS) // 2
assert POOL_HALO == F32_SUBLANES

Q_SCALE = (HEAD_DIM ** -0.5) * math.log2(math.e)

BF16 = jnp.bfloat16
F32 = jnp.float32


def _dot(a, b):
    return jnp.dot(a, b, preferred_element_type=F32)


def _dot_nt(a, b):
    return lax.dot_general(a, b, (((1,), (1,)), ((), ())), preferred_element_type=F32)


def _rms(x, gain):
    ms = jnp.mean(x * x, axis=-1, keepdims=True)
    return x * lax.rsqrt(ms + EPS) * gain


def _norm_mod(x, gain, shift, scale):
    return _rms(x, gain * (1.0 + scale)) + shift


def _params(*sem):
    return pltpu.CompilerParams(dimension_semantics=sem, vmem_limit_bytes=VMEM_LIMIT_BYTES)


def _adaln_kernel(c_ref, w_ref, b_ref, o_ref):
    c = c_ref[...]
    a = (c * jax.nn.sigmoid(c)).astype(BF16)
    o_ref[...] = _dot(a, w_ref[...].astype(BF16)) + b_ref[...]


def _adaln(cond, w_mod, b_mod):
    rows = cond.shape[0]
    n_out = w_mod.shape[1]
    bn = D_MODEL
    return pl.pallas_call(
        _adaln_kernel,
        grid=(n_out // bn,),
        in_specs=[pl.BlockSpec((rows, D_MODEL), lambda j: (0, 0)),
                  pl.BlockSpec((D_MODEL, bn), lambda j: (0, j)),
                  pl.BlockSpec((1, bn), lambda j: (0, j))],
        out_specs=pl.BlockSpec((rows, bn), lambda j: (0, j)),
        out_shape=jax.ShapeDtypeStruct((rows, n_out), F32),
        compiler_params=_params("parallel"),
        name="adaln",
    )(cond, w_mod, b_mod.reshape(1, n_out))


def _ctx_kv_kernel(x_ref, mod_ref, gpre_ref, w_ref, gk_ref, k_ref, v_ref):
    h = _norm_mod(x_ref[0], gpre_ref[...], mod_ref[0, 0:1, :], mod_ref[0, 1:2, :]).astype(BF16)
    pk = _dot(h, w_ref[:, :KV_WIDTH])
    pv = _dot(h, w_ref[:, KV_WIDTH:])
    for e in range(N_KV_HEADS):
        sl = slice(e * HEAD_DIM, (e + 1) * HEAD_DIM)
        k_ref[0, e] = _rms(pk[:, sl], gk_ref[...]).astype(BF16)
        v_ref[0, e] = pv[:, sl].astype(BF16)


def _ctx_kv(ctx, mod_ctx, g_pre, w_kv, g_k):
    b, n, _ = ctx.shape
    kv_shape = jax.ShapeDtypeStruct((b, N_KV_HEADS, n, HEAD_DIM), BF16)
    kv_spec = pl.BlockSpec((1, N_KV_HEADS, n, HEAD_DIM), lambda i: (i, 0, 0, 0))
    return pl.pallas_call(
        _ctx_kv_kernel,
        grid=(b,),
        in_specs=[pl.BlockSpec((1, n, D_MODEL), lambda i: (i, 0, 0)),
                  pl.BlockSpec((1, N_MOD, D_MODEL), lambda i: (0, 0, 0)),
                  pl.BlockSpec((1, D_MODEL), lambda i: (0, 0)),
                  pl.BlockSpec((D_MODEL, 2 * KV_WIDTH), lambda i: (0, 0)),
                  pl.BlockSpec((1, HEAD_DIM), lambda i: (0, 0))],
        out_specs=[kv_spec, kv_spec],
        out_shape=[kv_shape, kv_shape],
        compiler_params=_params("parallel"),
        name="ctx_kv",
    )(ctx, mod_ctx, g_pre, w_kv, g_k)


def _pool_fold_kernel(wg_ref, ps_ref, wu_ref, o_ref):
    for g in range(len(POOL_WINDOWS)):
        sl = slice(g * POOL_GROUP_WIDTH, (g + 1) * POOL_GROUP_WIDTH)
        o_ref[sl, :] = jnp.dot(wg_ref[g] * ps_ref[:, sl], wu_ref[sl, :], precision=lax.Precision.HIGHEST,
                               preferred_element_type=F32).astype(BF16)


def _pool_fold(w_grp, pool_scale, w_pool_up):
    return pl.pallas_call(
        _pool_fold_kernel,
        out_shape=jax.ShapeDtypeStruct(w_pool_up.shape, BF16),
        compiler_params=pltpu.CompilerParams(vmem_limit_bytes=VMEM_LIMIT_BYTES),
        name="pool_fold",
    )(w_grp, pool_scale, w_pool_up)


def _rope_fold_kernel(cos_ref, sin_ref, gq_ref, gk_ref, cq_ref, sq_ref, ck_ref, sk_ref):
    lane = lax.broadcasted_iota(jnp.int32, (F32_SUBLANES, HEAD_DIM), 1)
    low_half = (lane & ROPE_PARTNER) == 0
    cos = cos_ref[...]
    sin = sin_ref[...]
    for g_ref, c_ref, s_ref, logit_scale in ((gq_ref, cq_ref, sq_ref, Q_SCALE), (gk_ref, ck_ref, sk_ref, 1.0)):
        g = jnp.broadcast_to(g_ref[...], (F32_SUBLANES, HEAD_DIM)) * logit_scale
        g_partner = jnp.where(low_half, pltpu.roll(g, HEAD_DIM - ROPE_PARTNER, 1),
                              pltpu.roll(g, ROPE_PARTNER, 1))
        c_ref[...] = cos * g[0:1, :]
        s_ref[...] = sin * g_partner[0:1, :]


def _rope_fold(cos_t, sin_t, g_q, g_k):
    tab = jax.ShapeDtypeStruct(cos_t.shape, F32)
    return pl.pallas_call(
        _rope_fold_kernel,
        out_shape=[tab] * 4,
        compiler_params=pltpu.CompilerParams(vmem_limit_bytes=VMEM_LIMIT_BYTES),
        name="rope_fold",
    )(cos_t, sin_t, g_q, g_k)


def _in_proj_kernel(x_ref, xp_ref, xn_ref, mod_ref, gpre_ref, w_ref, bg_ref,
                    cq_ref, sq_ref, ck_ref, sk_ref, q_ref, k_ref, v_ref, pool_ref, gate_ref,
                    *, tm, sub, n_tiles, seq):
    i = pl.program_id(1)
    gpre = gpre_ref[...]
    shift = mod_ref[0, 0:1, :]
    scale = mod_ref[0, 1:2, :]
    subs = [slice(s * sub, (s + 1) * sub) for s in range(tm // sub)]
    hb = [_norm_mod(x_ref[0, rs, :], gpre, shift, scale).astype(BF16) for rs in subs]

    w_pool = w_ref[:, POOL_OFF:POOL_OFF + POOL_WIDTH]
    lane = lax.broadcasted_iota(jnp.int32, (sub, HEAD_DIM), 1)
    low_half = (lane & ROPE_PARTNER) == 0
    heads_per_dot = MXU_WIDTH // HEAD_DIM
    gate_chunk = GATE_WIDTH // (ATTN_WIDTH // MXU_WIDTH)

    def norm_rope(t, cos_g, sin_g):
        inv = lax.rsqrt(jnp.mean(t * t, axis=-1, keepdims=True) + EPS)
        partner = jnp.where(low_half, pltpu.roll(t, HEAD_DIM - ROPE_PARTNER, 1),
                            pltpu.roll(t, ROPE_PARTNER, 1))
        return ((t * cos_g + partner * sin_g) * inv).astype(BF16)

    def pool_windows(u_sub):
        u_main = jnp.concatenate(u_sub, axis=0)
        x_halo = jnp.concatenate([xp_ref[0], xn_ref[0]], axis=0)
        u_halo = _dot(_norm_mod(x_halo, gpre, shift, scale).astype(BF16), w_pool)
        u_prev = jnp.where(i > 0, u_halo[:POOL_HALO], 0.0)
        u_next = jnp.where(i < n_tiles - 1, u_halo[POOL_HALO:], 0.0)
        u_ext = jnp.concatenate([u_prev, u_main, u_next], axis=0)
        n_ext = tm + 2 * POOL_HALO
        t_glob = i * tm + lax.broadcasted_iota(jnp.int32, (tm, POOL_GROUP_WIDTH), 0)
        for gi, win in enumerate(POOL_WINDOWS):
            sl = slice(gi * POOL_GROUP_WIDTH, (gi + 1) * POOL_GROUP_WIDTH)
            half = win // 2
            run = u_ext[:, sl]
            length = 1
            while length < half:
                run = run + pltpu.roll(run, n_ext - length, 0)
                length *= 2
            window = run + pltpu.roll(run, half, 0)
            window = window[POOL_HALO:POOL_HALO + tm]
            cnt = (jnp.minimum(t_glob + half, seq) - jnp.maximum(t_glob - half, 0)).astype(F32)
            pool_ref[0, :, sl] = (window / cnt - u_main[:, sl]).astype(BF16)

    u_sub = []
    for rs, h in zip(subs, hb):
        u_sub.append(_dot(h, w_pool))
        cq, sq = cq_ref[rs, :], sq_ref[rs, :]
        for j in range(ATTN_WIDTH // MXU_WIDTH):
            pq = _dot(h, w_ref[:, j * MXU_WIDTH:(j + 1) * MXU_WIDTH])
            for e in range(heads_per_dot):
                t = pq[:, e * HEAD_DIM:(e + 1) * HEAD_DIM]
                q_ref[0, j * heads_per_dot + e, rs, :] = norm_rope(t, cq, sq)
        pk = _dot(h, w_ref[:, K_OFF:K_OFF + KV_WIDTH])
        pv = _dot(h, w_ref[:, V_OFF:V_OFF + KV_WIDTH])
        for e in range(N_KV_HEADS):
            sl = slice(e * HEAD_DIM, (e + 1) * HEAD_DIM)
            k_ref[0, e, rs, :] = norm_rope(pk[:, sl], ck_ref[rs, :], sk_ref[rs, :])
            v_ref[0, e, rs, :] = pv[:, sl].astype(BF16)
    pool_windows(u_sub)
    for rs, h in zip(subs, hb):
        for j in range(GATE_WIDTH // gate_chunk):
            sl = slice(j * gate_chunk, (j + 1) * gate_chunk)
            logits = _dot(h, w_ref[:, GATE_OFF + j * gate_chunk:GATE_OFF + (j + 1) * gate_chunk])
            gate_ref[0, rs, sl] = jax.nn.sigmoid(logits + bg_ref[:, sl]).astype(BF16)


def _in_proj(x, mod, g_pre, w_in, b_gate, rope_tabs, *, tm, sub):
    b, s, _ = x.shape
    n_tiles = s // tm
    halo_blocks_per_tile = tm // POOL_HALO
    last_halo_block = s // POOL_HALO - 1
    const2 = lambda bi, i: (0, 0)
    in_specs = [
        pl.BlockSpec((1, tm, D_MODEL), lambda bi, i: (bi, i, 0)),
        pl.BlockSpec((1, POOL_HALO, D_MODEL),
                     lambda bi, i: (bi, jnp.maximum(i * halo_blocks_per_tile - 1, 0), 0)),
        pl.BlockSpec((1, POOL_HALO, D_MODEL),
                     lambda bi, i: (bi, jnp.minimum((i + 1) * halo_blocks_per_tile, last_halo_block), 0)),
        pl.BlockSpec((1, N_MOD, D_MODEL), lambda bi, i: (bi, 0, 0)),
        pl.BlockSpec((1, D_MODEL), const2),
        pl.BlockSpec(w_in.shape, const2, pipeline_mode=pl.Buffered(1)),
        pl.BlockSpec((1, GATE_WIDTH), const2),
    ] + [pl.BlockSpec((tm, HEAD_DIM), lambda bi, i: (i, 0))] * len(rope_tabs)
    out_specs = [
        pl.BlockSpec((1, N_Q_HEADS, tm, HEAD_DIM), lambda bi, i: (bi, 0, i, 0)),
        pl.BlockSpec((1, N_KV_HEADS, tm, HEAD_DIM), lambda bi, i: (bi, 0, i, 0)),
        pl.BlockSpec((1, N_KV_HEADS, tm, HEAD_DIM), lambda bi, i: (bi, 0, i, 0)),
        pl.BlockSpec((1, tm, POOL_WIDTH), lambda bi, i: (bi, i, 0)),
        pl.BlockSpec((1, tm, GATE_WIDTH), lambda bi, i: (bi, i, 0)),
    ]
    out_shape = [
        jax.ShapeDtypeStruct((b, N_Q_HEADS, s, HEAD_DIM), BF16),
        jax.ShapeDtypeStruct((b, N_KV_HEADS, s, HEAD_DIM), BF16),
        jax.ShapeDtypeStruct((b, N_KV_HEADS, s, HEAD_DIM), BF16),
        jax.ShapeDtypeStruct((b, s, POOL_WIDTH), BF16),
        jax.ShapeDtypeStruct((b, s, GATE_WIDTH), BF16),
    ]
    return pl.pallas_call(
        functools.partial(_in_proj_kernel, tm=tm, sub=sub, n_tiles=n_tiles, seq=s),
        grid=(b, n_tiles),
        in_specs=in_specs,
        out_specs=out_specs,
        out_shape=out_shape,
        compiler_params=_params("parallel", "parallel"),
        name="in_proj",
    )(x, x, x, mod, g_pre, w_in, b_gate, *rope_tabs)


def _attention_kernel(q_ref, kc_ref, vc_ref, kl_ref, vl_ref, o_ref, *, tq, rows):
    kc = kc_ref[0, 0]
    kl = kl_ref[0, 0]
    ones_cols = lambda n: jnp.ones((n, MXU_WIDTH - HEAD_DIM), BF16)
    vc = jnp.concatenate([vc_ref[0, 0], ones_cols(kc.shape[0])], axis=-1)
    vl = jnp.concatenate([vl_ref[0, 0], ones_cols(kl.shape[0])], axis=-1)
    for h in range(Q_GROUP):
        for r in range(tq // rows):
            rs = slice(r * rows, (r + 1) * rows)
            q = q_ref[0, 0, h, rs, :]
            s_c = _dot_nt(q, kc)
            s_l = _dot_nt(q, kl)
            m = jnp.maximum(jnp.max(s_c, axis=-1, keepdims=True),
                            jnp.max(s_l, axis=-1, keepdims=True))
            p_c = jnp.exp2(s_c - m).astype(BF16)
            p_l = jnp.exp2(s_l - m).astype(BF16)
            o = _dot(p_c, vc) + _dot(p_l, vl)
            o_ref[0, rs, h * HEAD_DIM:(h + 1) * HEAD_DIM] = (
                o[:, :HEAD_DIM] / o[:, HEAD_DIM:]).astype(BF16)


def _attention(q, k_ctx, v_ctx, k_lat, v_lat, *, tq, rows):
    b, _, s, _ = q.shape
    n_ctx = k_ctx.shape[2]
    qg = q.reshape(b, N_KV_HEADS, Q_GROUP, s, HEAD_DIM)
    kv_idx = lambda bi, kh, i: (bi, kh, 0, 0)
    return pl.pallas_call(
        functools.partial(_attention_kernel, tq=tq, rows=rows),
        grid=(b, N_KV_HEADS, s // tq),
        in_specs=[pl.BlockSpec((1, 1, Q_GROUP, tq, HEAD_DIM), lambda bi, kh, i: (bi, kh, 0, i, 0)),
                  pl.BlockSpec((1, 1, n_ctx, HEAD_DIM), kv_idx),
                  pl.BlockSpec((1, 1, n_ctx, HEAD_DIM), kv_idx),
                  pl.BlockSpec((1, 1, s, HEAD_DIM), kv_idx),
                  pl.BlockSpec((1, 1, s, HEAD_DIM), kv_idx)],
        out_specs=pl.BlockSpec((1, tq, Q_GROUP * HEAD_DIM), lambda bi, kh, i: (bi, i, kh)),
        out_shape=jax.ShapeDtypeStruct((b, s, ATTN_WIDTH), BF16),
        compiler_params=_params("parallel", "parallel", "parallel"),
        name="attention",
    )(qg, k_ctx, v_ctx, k_lat, v_lat)


def _merge_mlp_kernel(x_ref, a_ref, p_ref, g_ref, mod_ref, gmix_ref, gpre_ref, gpost_ref,
                      wa_ref, wp_ref, wo_ref, w1_ref, w2_ref, o_ref, *, tm, sub, ff_chunk):
    subs = [slice(s * sub, (s + 1) * sub) for s in range(tm // sub)]
    x_mid = []
    for rs in subs:
        ya = _dot(a_ref[0, rs, :], wa_ref[...])
        yp = _dot(p_ref[0, rs, :], wp_ref[...])
        ga = g_ref[0, rs, :D_MODEL].astype(F32)
        gp = g_ref[0, rs, D_MODEL:].astype(F32)
        y = _dot((ga * ya + gp * yp).astype(BF16), wo_ref[...])
        x_mid.append(x_ref[0, rs, :] + mod_ref[0, 2:3, :] * _rms(y, gmix_ref[...]))
    for rs, xs in zip(subs, x_mid):
        h = _norm_mod(xs, gpre_ref[...], mod_ref[0, 3:4, :], mod_ref[0, 4:5, :]).astype(BF16)
        acc = None
        for j in range(D_FF // ff_chunk):
            sl = slice(j * ff_chunk, (j + 1) * ff_chunk)
            t = jnp.maximum(_dot(h, w1_ref[:, sl]), 0.0)
            part = _dot((t * t).astype(BF16), w2_ref[sl, :])
            acc = part if acc is None else acc + part
        o_ref[0, rs, :] = xs + mod_ref[0, 5:6, :] * _rms(acc, gpost_ref[...])


def _merge_mlp(x, attn_o, pooled, gates, mod, g_post_mix, g_pre_mlp, g_post_mlp,
               w_attn_up, w_pool_fold, w_out, w_ff1, w_ff2, *, tm, sub, ff_chunk):
    b, s, _ = x.shape
    tile = lambda width: pl.BlockSpec((1, tm, width), lambda bi, i: (bi, i, 0))
    const2 = lambda bi, i: (0, 0)
    gain = pl.BlockSpec((1, D_MODEL), const2)
    weight = lambda w: pl.BlockSpec(w.shape, const2, pipeline_mode=pl.Buffered(1))
    return pl.pallas_call(
        functools.partial(_merge_mlp_kernel, tm=tm, sub=sub, ff_chunk=ff_chunk),
        grid=(b, s // tm),
        in_specs=[tile(D_MODEL), tile(ATTN_WIDTH), tile(POOL_WIDTH), tile(GATE_WIDTH),
                  pl.BlockSpec((1, N_MOD, D_MODEL), lambda bi, i: (bi, 0, 0)),
                  gain, gain, gain,
                  weight(w_attn_up), weight(w_pool_fold), weight(w_out), weight(w_ff1), weight(w_ff2)],
        out_specs=tile(D_MODEL),
        out_shape=jax.ShapeDtypeStruct(x.shape, F32),
        compiler_params=_params("parallel", "parallel"),
        name="merge_mlp",
    )(x, attn_o, pooled, gates, mod, g_post_mix, g_pre_mlp, g_post_mlp,
      w_attn_up, w_pool_fold, w_out, w_ff1, w_ff2)


def _rope_tables(seq):
    t = jnp.arange(seq)
    rows = (t // GRID_W).astype(F32)
    cols = (t % GRID_W).astype(F32)
    n_freq = HEAD_DIM // 4
    freqs = ROPE_THETA ** (-jnp.arange(n_freq, dtype=F32) / n_freq)
    ang_row = rows[:, None] * freqs
    ang_col = cols[:, None] * freqs
    cos_r, sin_r, cos_c, sin_c = jnp.cos(ang_row), jnp.sin(ang_row), jnp.cos(ang_col), jnp.sin(ang_col)
    cos_t = jnp.concatenate([cos_r, cos_r, cos_c, cos_c], axis=-1)
    sin_t = jnp.concatenate([-sin_r, sin_r, -sin_c, sin_c], axis=-1)
    return cos_t, sin_t


def kernel(x, c, ctx, c_ctx, w_mod, b_mod, g_pre_mix, g_post_mix, g_pre_mlp, g_post_mlp, w_in, b_gate, g_q, g_k, w_attn_up, w_pool_grp, pool_scale, w_pool_up, w_out, w_ff1, w_ff2):
    depth = w_mod.shape[0]
    assert depth == 1, "single-layer block"
    b, s, _ = x.shape
    row = lambda a: a[0].reshape(1, -1)

    n_cond = -(-(b + 1) // F32_SUBLANES) * F32_SUBLANES
    cond = jnp.concatenate([c, c_ctx[None, :], jnp.zeros((n_cond - b - 1, D_MODEL), F32)], axis=0)
    mod_all = _adaln(cond, w_mod[0], b_mod[0])
    mod = mod_all[:b].reshape(b, N_MOD, D_MODEL)
    mod_ctx = mod_all[b:b + 1].reshape(1, N_MOD, D_MODEL)

    w_in_b = w_in[0].astype(BF16)
    cos_t, sin_t = _rope_tables(s)

    k_ctx, v_ctx = _ctx_kv(ctx, mod_ctx, row(g_pre_mix), w_in_b[:, K_OFF:POOL_OFF], row(g_k))
    rope_tabs = _rope_fold(cos_t, sin_t, row(g_q), row(g_k))
    q, k_lat, v_lat, pooled, gates = _in_proj(
        x, mod, row(g_pre_mix), w_in_b, row(b_gate), rope_tabs, tm=512, sub=256)
    attn_o = _attention(q, k_ctx, v_ctx, k_lat, v_lat, tq=1024, rows=128)
    w_pool_fold = _pool_fold(w_pool_grp[0], row(pool_scale), w_pool_up[0])
    return _merge_mlp(x, attn_o, pooled, gates, mod, row(g_post_mix), row(g_pre_mlp), row(g_post_mlp),
                      w_attn_up[0].astype(BF16), w_pool_fold, w_out[0].astype(BF16),
                      w_ff1[0].astype(BF16), w_ff2[0].astype(BF16), tm=512, sub=256, ff_chunk=1024)
```

```python
import functools
import math

import jax
import jax.numpy as jnp
from jax import lax
from jax.experimental import pallas as pl
from jax.experimental.pallas import tpu as pltpu

D_MODEL = 1024
GRID_W = 64
HEAD_DIM = 128
N_Q_HEADS = 8
N_KV_HEADS = 2
Q_GROUP = N_Q_HEADS // N_KV_HEADS
ATTN_WIDTH = N_Q_HEADS * HEAD_DIM
KV_WIDTH = N_KV_HEADS * HEAD_DIM
POOL_WINDOWS = (2, 4, 8, 16)
POOL_WIDTH = 512
POOL_GROUP_WIDTH = POOL_WIDTH // len(POOL_WINDOWS)
GATE_WIDTH = 2 * D_MODEL
K_OFF = ATTN_WIDTH
V_OFF = K_OFF + KV_WIDTH
POOL_OFF = V_OFF + KV_WIDTH
GATE_OFF = POOL_OFF + POOL_WIDTH
D_FF = 4 * D_MODEL
ROPE_THETA = 10000.0
ROPE_PARTNER = HEAD_DIM // 4
EPS = 1e-6
N_MOD = 6

F32_SUBLANES = 8
MXU_WIDTH = 256
VMEM_LIMIT_BYTES = 56 * 1024 * 1024

POOL_HALO = max(POOL_WINDOWS) // 2
assert POOL_HALO == F32_SUBLANES

Q_SCALE = (HEAD_DIM ** -0.5) * math.log2(math.e)

BF16 = jnp.bfloat16
F32 = jnp.float32


def _dot(a, b):
    return jnp.dot(a, b, preferred_element_type=F32)


def _dot_nt(a, b):
    return lax.dot_general(a, b, (((1,), (1,)), ((), ())), preferred_element_type=F32)


def _rms(x, gain):
    ms = jnp.mean(x * x, axis=-1, keepdims=True)
    return x * lax.rsqrt(ms + EPS) * gain


def _norm_mod(x, gain, shift, scale):
    return _rms(x, gain * (1.0 + scale)) + shift


def _params(*sem):
    return pltpu.CompilerParams(dimension_semantics=sem, vmem_limit_bytes=VMEM_LIMIT_BYTES)


def _adaln_kernel(c_ref, w_ref, b_ref, o_ref):
    c = c_ref[...]
    a = (c * jax.nn.sigmoid(c)).astype(BF16)
    o_ref[...] = _dot(a, w_ref[...].astype(BF16)) + b_ref[...]


def _adaln(cond, w_mod, b_mod):
    rows = cond.shape[0]
    n_out = w_mod.shape[1]
    bn = D_MODEL
    return pl.pallas_call(
        _adaln_kernel,
        grid=(n_out // bn,),
        in_specs=[pl.BlockSpec((rows, D_MODEL), lambda j: (0, 0)),
                  pl.BlockSpec((D_MODEL, bn), lambda j: (0, j)),
                  pl.BlockSpec((1, bn), lambda j: (0, j))],
        out_specs=pl.BlockSpec((rows, bn), lambda j: (0, j)),
        out_shape=jax.ShapeDtypeStruct((rows, n_out), F32),
        compiler_params=_params("parallel"),
        name="adaln",
    )(cond, w_mod, b_mod.reshape(1, n_out))


def _ctx_kv_kernel(x_ref, mod_ref, gpre_ref, w_ref, gk_ref, k_ref, v_ref):
    h = _norm_mod(x_ref[0], gpre_ref[...], mod_ref[0, 0:1, :], mod_ref[0, 1:2, :]).astype(BF16)
    pk = _dot(h, w_ref[:, :KV_WIDTH])
    pv = _dot(h, w_ref[:, KV_WIDTH:])
    for e in range(N_KV_HEADS):
        sl = slice(e * HEAD_DIM, (e + 1) * HEAD_DIM)
        k_ref[0, e] = _rms(pk[:, sl], gk_ref[...]).astype(BF16)
        v_ref[0, e] = pv[:, sl].astype(BF16)


def _ctx_kv(ctx, mod_ctx, g_pre, w_kv, g_k):
    b, n, _ = ctx.shape
    kv_shape = jax.ShapeDtypeStruct((b, N_KV_HEADS, n, HEAD_DIM), BF16)
    kv_spec = pl.BlockSpec((1, N_KV_HEADS, n, HEAD_DIM), lambda i: (i, 0, 0, 0))
    return pl.pallas_call(
        _ctx_kv_kernel,
        grid=(b,),
        in_specs=[pl.BlockSpec((1, n, D_MODEL), lambda i: (i, 0, 0)),
                  pl.BlockSpec((1, N_MOD, D_MODEL), lambda i: (0, 0, 0)),
                  pl.BlockSpec((1, D_MODEL), lambda i: (0, 0)),
                  pl.BlockSpec((D_MODEL, 2 * KV_WIDTH), lambda i: (0, 0)),
                  pl.BlockSpec((1, HEAD_DIM), lambda i: (0, 0))],
        out_specs=[kv_spec, kv_spec],
        out_shape=[kv_shape, kv_shape],
        compiler_params=_params("parallel"),
        name="ctx_kv",
    )(ctx, mod_ctx, g_pre, w_kv, g_k)


def _pool_fold_kernel(wg_ref, ps_ref, wu_ref, o_ref):
    for g in range(len(POOL_WINDOWS)):
        sl = slice(g * POOL_GROUP_WIDTH, (g + 1) * POOL_GROUP_WIDTH)
        o_ref[sl, :] = jnp.dot(wg_ref[g] * ps_ref[:, sl], wu_ref[sl, :], precision=lax.Precision.HIGHEST,
                               preferred_element_type=F32).astype(BF16)


def _pool_fold(w_grp, pool_scale, w_pool_up):
    return pl.pallas_call(
        _pool_fold_kernel,
        out_shape=jax.ShapeDtypeStruct(w_pool_up.shape, BF16),
        compiler_params=pltpu.CompilerParams(vmem_limit_bytes=VMEM_LIMIT_BYTES),
        name="pool_fold",
    )(w_grp, pool_scale, w_pool_up)


def _rope_fold_kernel(cos_ref, sin_ref, gq_ref, gk_ref, cq_ref, sq_ref, ck_ref, sk_ref):
    lane = lax.broadcasted_iota(jnp.int32, (F32_SUBLANES, HEAD_DIM), 1)
    low_half = (lane & ROPE_PARTNER) == 0
    cos = cos_ref[...]
    sin = sin_ref[...]
    for g_ref, c_ref, s_ref, logit_scale in ((gq_ref, cq_ref, sq_ref, Q_SCALE), (gk_ref, ck_ref, sk_ref, 1.0)):
        g = jnp.broadcast_to(g_ref[...], (F32_SUBLANES, HEAD_DIM)) * (logit_scale * math.sqrt(HEAD_DIM))
        g_partner = jnp.where(low_half, pltpu.roll(g, HEAD_DIM - ROPE_PARTNER, 1),
                              pltpu.roll(g, ROPE_PARTNER, 1))
        c_ref[...] = cos * g[0:1, :]
        s_ref[...] = sin * g_partner[0:1, :]


def _rope_fold(cos_t, sin_t, g_q, g_k):
    tab = jax.ShapeDtypeStruct(cos_t.shape, F32)
    return pl.pallas_call(
        _rope_fold_kernel,
        out_shape=[tab] * 4,
        compiler_params=pltpu.CompilerParams(vmem_limit_bytes=VMEM_LIMIT_BYTES),
        name="rope_fold",
    )(cos_t, sin_t, g_q, g_k)


def _in_proj_kernel(x_ref, xp_ref, xn_ref, mod_ref, gpre_ref, w_ref, bg_ref,
                    cq_ref, sq_ref, ck_ref, sk_ref, q_ref, k_ref, v_ref, pool_ref, gate_ref,
                    *, tm, sub, n_tiles, seq):
    i = pl.program_id(1)
    gpre = gpre_ref[...]
    shift = mod_ref[0, 0:1, :]
    scale = mod_ref[0, 1:2, :]
    subs = [slice(s * sub, (s + 1) * sub) for s in range(tm // sub)]
    hb = [_norm_mod(x_ref[0, rs, :], gpre, shift, scale).astype(BF16) for rs in subs]

    w_pool = w_ref[:, POOL_OFF:POOL_OFF + POOL_WIDTH]
    lane = lax.broadcasted_iota(jnp.int32, (sub, HEAD_DIM), 1)
    low_half = (lane & ROPE_PARTNER) == 0
    heads_per_dot = MXU_WIDTH // HEAD_DIM
    gate_chunk = GATE_WIDTH // (ATTN_WIDTH // MXU_WIDTH)

    def norm_rope(t, cos_g, sin_g):
        inv = lax.rsqrt(jnp.sum(t * t, axis=-1, keepdims=True) + HEAD_DIM * EPS)
        partner = jnp.where(low_half, pltpu.roll(t, HEAD_DIM - ROPE_PARTNER, 1),
                            pltpu.roll(t, ROPE_PARTNER, 1))
        return ((t * cos_g + partner * sin_g) * inv).astype(BF16)

    def pool_windows(u_sub):
        u_main = jnp.concatenate(u_sub, axis=0)
        x_halo = jnp.concatenate([xp_ref[0], xn_ref[0]], axis=0)
        u_halo = _dot(_norm_mod(x_halo, gpre, shift, scale).astype(BF16), w_pool)
        u_prev = jnp.where(i > 0, u_halo[:POOL_HALO], 0.0)
        u_next = jnp.where(i < n_tiles - 1, u_halo[POOL_HALO:], 0.0)
        u_ext = jnp.concatenate([u_prev, u_main, u_next], axis=0)
        n_ext = tm + 2 * POOL_HALO
        edge_iota = lax.broadcasted_iota(jnp.int32, (POOL_HALO, POOL_GROUP_WIDTH), 0)
        edge_t = (i * tm + edge_iota, i * tm + (tm - POOL_HALO) + edge_iota)
        for gi, win in enumerate(POOL_WINDOWS):
            sl = slice(gi * POOL_GROUP_WIDTH, (gi + 1) * POOL_GROUP_WIDTH)
            half = win // 2
            run = u_ext[:, sl]
            length = 1
            while length < half:
                run = run + pltpu.roll(run, n_ext - length, 0)
                length *= 2
            window = run + pltpu.roll(run, half, 0)
            window = window[POOL_HALO:POOL_HALO + tm]
            inv_cnt = [1.0 / (jnp.minimum(t + half, seq) - jnp.maximum(t - half, 0)).astype(F32)
                       for t in edge_t]
            inv_cnt = jnp.concatenate(
                [inv_cnt[0], jnp.full((tm - 2 * POOL_HALO, POOL_GROUP_WIDTH), 1.0 / win, F32), inv_cnt[1]],
                axis=0)
            pool_ref[0, :, sl] = (window * inv_cnt - u_main[:, sl]).astype(BF16)

    u_sub = []
    for rs, h in zip(subs, hb):
        u_sub.append(_dot(h, w_pool))
        cq, sq = cq_ref[rs, :], sq_ref[rs, :]
        for j in range(ATTN_WIDTH // MXU_WIDTH):
            pq = _dot(h, w_ref[:, j * MXU_WIDTH:(j + 1) * MXU_WIDTH])
            for e in range(heads_per_dot):
                t = pq[:, e * HEAD_DIM:(e + 1) * HEAD_DIM]
                q_ref[0, j * heads_per_dot + e, rs, :] = norm_rope(t, cq, sq)
        pk = _dot(h, w_ref[:, K_OFF:K_OFF + KV_WIDTH])
        pv = _dot(h, w_ref[:, V_OFF:V_OFF + KV_WIDTH])
        for e in range(N_KV_HEADS):
            sl = slice(e * HEAD_DIM, (e + 1) * HEAD_DIM)
            k_ref[0, e, rs, :] = norm_rope(pk[:, sl], ck_ref[rs, :], sk_ref[rs, :])
            v_ref[0, e, rs, :] = pv[:, sl].astype(BF16)
    pool_windows(u_sub)
    for rs, h in zip(subs, hb):
        for j in range(GATE_WIDTH // gate_chunk):
            sl = slice(j * gate_chunk, (j + 1) * gate_chunk)
            logits = _dot(h, w_ref[:, GATE_OFF + j * gate_chunk:GATE_OFF + (j + 1) * gate_chunk])
            gate_ref[0, rs, sl] = jax.nn.sigmoid(logits + bg_ref[:, sl]).astype(BF16)


def _in_proj(x, mod, g_pre, w_in, b_gate, rope_tabs, *, tm, sub):
    b, s, _ = x.shape
    n_tiles = s // tm
    halo_blocks_per_tile = tm // POOL_HALO
    last_halo_block = s // POOL_HALO - 1
    const2 = lambda bi, i: (0, 0)
    in_specs = [
        pl.BlockSpec((1, tm, D_MODEL), lambda bi, i: (bi, i, 0)),
        pl.BlockSpec((1, POOL_HALO, D_MODEL),
                     lambda bi, i: (bi, jnp.maximum(i * halo_blocks_per_tile - 1, 0), 0)),
        pl.BlockSpec((1, POOL_HALO, D_MODEL),
                     lambda bi, i: (bi, jnp.minimum((i + 1) * halo_blocks_per_tile, last_halo_block), 0)),
        pl.BlockSpec((1, N_MOD, D_MODEL), lambda bi, i: (bi, 0, 0)),
        pl.BlockSpec((1, D_MODEL), const2),
        pl.BlockSpec(w_in.shape, const2),
        pl.BlockSpec((1, GATE_WIDTH), const2),
    ] + [pl.BlockSpec((tm, HEAD_DIM), lambda bi, i: (i, 0))] * len(rope_tabs)
    out_specs = [
        pl.BlockSpec((1, N_Q_HEADS, tm, HEAD_DIM), lambda bi, i: (bi, 0, i, 0)),
        pl.BlockSpec((1, N_KV_HEADS, tm, HEAD_DIM), lambda bi, i: (bi, 0, i, 0)),
        pl.BlockSpec((1, N_KV_HEADS, tm, HEAD_DIM), lambda bi, i: (bi, 0, i, 0)),
        pl.BlockSpec((1, tm, POOL_WIDTH), lambda bi, i: (bi, i, 0)),
        pl.BlockSpec((1, tm, GATE_WIDTH), lambda bi, i: (bi, i, 0)),
    ]
    out_shape = [
        jax.ShapeDtypeStruct((b, N_Q_HEADS, s, HEAD_DIM), BF16),
        jax.ShapeDtypeStruct((b, N_KV_HEADS, s, HEAD_DIM), BF16),
        jax.ShapeDtypeStruct((b, N_KV_HEADS, s, HEAD_DIM), BF16),
        jax.ShapeDtypeStruct((b, s, POOL_WIDTH), BF16),
        jax.ShapeDtypeStruct((b, s, GATE_WIDTH), BF16),
    ]
    return pl.pallas_call(
        functools.partial(_in_proj_kernel, tm=tm, sub=sub, n_tiles=n_tiles, seq=s),
        grid=(b, n_tiles),
        in_specs=in_specs,
        out_specs=out_specs,
        out_shape=out_shape,
        compiler_params=_params("parallel", "parallel"),
        name="in_proj",
    )(x, x, x, mod, g_pre, w_in, b_gate, *rope_tabs)


def _attention_kernel(q_ref, kc_ref, vc_ref, kl_ref, vl_ref, o_ref, *, tq, rows):
    kc = kc_ref[0, 0]
    kl = kl_ref[0, 0]
    ones_cols = lambda n: jnp.ones((n, MXU_WIDTH - HEAD_DIM), BF16)
    vc = jnp.concatenate([vc_ref[0, 0], ones_cols(kc.shape[0])], axis=-1)
    vl = jnp.concatenate([vl_ref[0, 0], ones_cols(kl.shape[0])], axis=-1)
    for h in range(Q_GROUP):
        for r in range(tq // rows):
            rs = slice(r * rows, (r + 1) * rows)
            q = q_ref[0, 0, h, rs, :]
            s_c = _dot_nt(q, kc)
            s_l = _dot_nt(q, kl)
            m = jnp.maximum(jnp.max(s_c, axis=-1, keepdims=True),
                            jnp.max(s_l, axis=-1, keepdims=True))
            p_c = jnp.exp2(s_c - m).astype(BF16)
            p_l = jnp.exp2(s_l - m).astype(BF16)
            o = _dot(p_c, vc) + _dot(p_l, vl)
            o_ref[0, rs, h * HEAD_DIM:(h + 1) * HEAD_DIM] = (
                o[:, :HEAD_DIM] / o[:, HEAD_DIM:]).astype(BF16)


def _attention(q, k_ctx, v_ctx, k_lat, v_lat, *, tq, rows):
    b, _, s, _ = q.shape
    n_ctx = k_ctx.shape[2]
    qg = q.reshape(b, N_KV_HEADS, Q_GROUP, s, HEAD_DIM)
    kv_idx = lambda bi, kh, i: (bi, kh, 0, 0)
    return pl.pallas_call(
        functools.partial(_attention_kernel, tq=tq, rows=rows),
        grid=(b, N_KV_HEADS, s // tq),
        in_specs=[pl.BlockSpec((1, 1, Q_GROUP, tq, HEAD_DIM), lambda bi, kh, i: (bi, kh, 0, i, 0)),
                  pl.BlockSpec((1, 1, n_ctx, HEAD_DIM), kv_idx),
                  pl.BlockSpec((1, 1, n_ctx, HEAD_DIM), kv_idx),
                  pl.BlockSpec((1, 1, s, HEAD_DIM), kv_idx),
                  pl.BlockSpec((1, 1, s, HEAD_DIM), kv_idx)],
        out_specs=pl.BlockSpec((1, tq, Q_GROUP * HEAD_DIM), lambda bi, kh, i: (bi, i, kh)),
        out_shape=jax.ShapeDtypeStruct((b, s, ATTN_WIDTH), BF16),
        compiler_params=_params("parallel", "parallel", "parallel"),
        name="attention",
    )(qg, k_ctx, v_ctx, k_lat, v_lat)


def _merge_mlp_kernel(x_ref, a_ref, p_ref, g_ref, mod_ref, gmix_ref, gpre_ref, gpost_ref,
                      wa_ref, wp_ref, wo_ref, w1_ref, w2_ref, o_ref, *, tm, sub, ff_chunk):
    subs = [slice(s * sub, (s + 1) * sub) for s in range(tm // sub)]
    x_mid = []
    for rs in subs:
        ya = _dot(a_ref[0, rs, :], wa_ref[...])
        yp = _dot(p_ref[0, rs, :], wp_ref[...])
        ga = g_ref[0, rs, :D_MODEL].astype(F32)
        gp = g_ref[0, rs, D_MODEL:].astype(F32)
        y = _dot((ga * ya + gp * yp).astype(BF16), wo_ref[...])
        x_mid.append(x_ref[0, rs, :] + mod_ref[0, 2:3, :] * _rms(y, gmix_ref[...]))
    for rs, xs in zip(subs, x_mid):
        h = _norm_mod(xs, gpre_ref[...], mod_ref[0, 3:4, :], mod_ref[0, 4:5, :]).astype(BF16)
        acc = None
        for j in range(D_FF // ff_chunk):
            sl = slice(j * ff_chunk, (j + 1) * ff_chunk)
            t = jnp.maximum(_dot(h, w1_ref[:, sl]), 0.0)
            part = _dot((t * t).astype(BF16), w2_ref[sl, :])
            acc = part if acc is None else acc + part
        o_ref[0, rs, :] = xs + mod_ref[0, 5:6, :] * _rms(acc, gpost_ref[...])


def _merge_mlp(x, attn_o, pooled, gates, mod, g_post_mix, g_pre_mlp, g_post_mlp,
               w_attn_up, w_pool_fold, w_out, w_ff1, w_ff2, *, tm, sub, ff_chunk):
    b, s, _ = x.shape
    tile = lambda width: pl.BlockSpec((1, tm, width), lambda bi, i: (bi, i, 0))
    const2 = lambda bi, i: (0, 0)
    gain = pl.BlockSpec((1, D_MODEL), const2)
    weight = lambda w: pl.BlockSpec(w.shape, const2, pipeline_mode=pl.Buffered(1))
    return pl.pallas_call(
        functools.partial(_merge_mlp_kernel, tm=tm, sub=sub, ff_chunk=ff_chunk),
        grid=(b, s // tm),
        in_specs=[tile(D_MODEL), tile(ATTN_WIDTH), tile(POOL_WIDTH), tile(GATE_WIDTH),
                  pl.BlockSpec((1, N_MOD, D_MODEL), lambda bi, i: (bi, 0, 0)),
                  gain, gain, gain,
                  weight(w_attn_up), weight(w_pool_fold), weight(w_out), weight(w_ff1), weight(w_ff2)],
        out_specs=tile(D_MODEL),
        out_shape=jax.ShapeDtypeStruct(x.shape, F32),
        compiler_params=_params("parallel", "parallel"),
        name="merge_mlp",
    )(x, attn_o, pooled, gates, mod, g_post_mix, g_pre_mlp, g_post_mlp,
      w_attn_up, w_pool_fold, w_out, w_ff1, w_ff2)


def _rope_tables(seq):
    t = jnp.arange(seq)
    rows = (t // GRID_W).astype(F32)
    cols = (t % GRID_W).astype(F32)
    n_freq = HEAD_DIM // 4
    freqs = ROPE_THETA ** (-jnp.arange(n_freq, dtype=F32) / n_freq)
    ang_row = rows[:, None] * freqs
    ang_col = cols[:, None] * freqs
    cos_r, sin_r, cos_c, sin_c = jnp.cos(ang_row), jnp.sin(ang_row), jnp.cos(ang_col), jnp.sin(ang_col)
    cos_t = jnp.concatenate([cos_r, cos_r, cos_c, cos_c], axis=-1)
    sin_t = jnp.concatenate([-sin_r, sin_r, -sin_c, sin_c], axis=-1)
    return cos_t, sin_t


def kernel(x, c, ctx, c_ctx, w_mod, b_mod, g_pre_mix, g_post_mix, g_pre_mlp, g_post_mlp, w_in, b_gate, g_q, g_k, w_attn_up, w_pool_grp, pool_scale, w_pool_up, w_out, w_ff1, w_ff2):
    depth = w_mod.shape[0]
    assert depth == 1, "single-layer block"
    b, s, _ = x.shape
    row = lambda a: a[0].reshape(1, -1)

    n_cond = -(-(b + 1) // F32_SUBLANES) * F32_SUBLANES
    cond = jnp.concatenate([c, c_ctx[None, :], jnp.zeros((n_cond - b - 1, D_MODEL), F32)], axis=0)
    mod_all = _adaln(cond, w_mod[0], b_mod[0])
    mod = mod_all[:b].reshape(b, N_MOD, D_MODEL)
    mod_ctx = mod_all[b:b + 1].reshape(1, N_MOD, D_MODEL)

    w_in_b = w_in[0].astype(BF16)
    cos_t, sin_t = _rope_tables(s)

    k_ctx, v_ctx = _ctx_kv(ctx, mod_ctx, row(g_pre_mix), w_in_b[:, K_OFF:POOL_OFF], row(g_k))
    rope_tabs = _rope_fold(cos_t, sin_t, row(g_q), row(g_k))
    q, k_lat, v_lat, pooled, gates = _in_proj(
        x, mod, row(g_pre_mix), w_in_b, row(b_gate), rope_tabs, tm=512, sub=256)
    attn_o = _attention(q, k_ctx, v_ctx, k_lat, v_lat, tq=1024, rows=128)
    w_pool_fold = _pool_fold(w_pool_grp[0], row(pool_scale), w_pool_up[0])
    return _merge_mlp(x, attn_o, pooled, gates, mod, row(g_post_mix), row(g_pre_mlp), row(g_post_mlp),
                      w_attn_up[0].astype(BF16), w_pool_fold, w_out[0].astype(BF16),
                      w_ff1[0].astype(BF16), w_ff2[0].astype(BF16), tm=512, sub=256, ff_chunk=1024)
```

```python
import functools
import math

import jax
import jax.numpy as jnp
from jax import lax
from jax.experimental import pallas as pl
from jax.experimental.pallas import tpu as pltpu

D_MODEL = 1024
GRID_W = 64
HEAD_DIM = 128
N_Q_HEADS = 8
N_KV_HEADS = 2
Q_GROUP = N_Q_HEADS // N_KV_HEADS
ATTN_WIDTH = N_Q_HEADS * HEAD_DIM
KV_WIDTH = N_KV_HEADS * HEAD_DIM
POOL_WINDOWS = (2, 4, 8, 16)
POOL_WIDTH = 512
POOL_GROUP_WIDTH = POOL_WIDTH // len(POOL_WINDOWS)
GATE_WIDTH = 2 * D_MODEL
K_OFF = ATTN_WIDTH
V_OFF = K_OFF + KV_WIDTH
POOL_OFF = V_OFF + KV_WIDTH
GATE_OFF = POOL_OFF + POOL_WIDTH
D_FF = 4 * D_MODEL
ROPE_THETA = 10000.0
ROPE_PARTNER = HEAD_DIM // 4
EPS = 1e-6
N_MOD = 6

F32_SUBLANES = 8
MXU_WIDTH = 256
VMEM_LIMIT_BYTES = 56 * 1024 * 1024

POOL_HALO = max(POOL_WINDOWS) // 2
assert POOL_HALO == F32_SUBLANES

Q_SCALE = (HEAD_DIM ** -0.5) * math.log2(math.e)

BF16 = jnp.bfloat16
F32 = jnp.float32


def _dot(a, b):
    return jnp.dot(a, b, preferred_element_type=F32)


def _dot_nt(a, b):
    return lax.dot_general(a, b, (((1,), (1,)), ((), ())), preferred_element_type=F32)


def _rms(x, gain):
    ms = jnp.mean(x * x, axis=-1, keepdims=True)
    return x * lax.rsqrt(ms + EPS) * gain


def _norm_mod(x, gain, shift, scale):
    return _rms(x, gain * (1.0 + scale)) + shift


_VMEM_RESIDENT = pl.BlockSpec(memory_space=pltpu.VMEM)


def _params(*sem):
    return pltpu.CompilerParams(dimension_semantics=sem, vmem_limit_bytes=VMEM_LIMIT_BYTES)


def _adaln_kernel(c_ref, w_ref, b_ref, o_ref):
    c = c_ref[...]
    a = (c * jax.nn.sigmoid(c)).astype(BF16)
    o_ref[...] = _dot(a, w_ref[...].astype(BF16)) + b_ref[...]


def _adaln(cond, w_mod, b_mod):
    rows = cond.shape[0]
    n_out = w_mod.shape[1]
    bn = D_MODEL
    return pl.pallas_call(
        _adaln_kernel,
        grid=(n_out // bn,),
        in_specs=[pl.BlockSpec((rows, D_MODEL), lambda j: (0, 0)),
                  pl.BlockSpec((D_MODEL, bn), lambda j: (0, j)),
                  pl.BlockSpec((1, bn), lambda j: (0, j))],
        out_specs=pl.BlockSpec((rows, bn), lambda j: (0, j)),
        out_shape=jax.ShapeDtypeStruct((rows, n_out), F32),
        compiler_params=_params("parallel"),
        name="adaln",
    )(cond, w_mod, b_mod.reshape(1, n_out))


def _ctx_kv_kernel(x_ref, mod_ref, gpre_ref, w_ref, gk_ref, k_ref, v_ref):
    for s in range(x_ref.shape[0]):
        h = _norm_mod(x_ref[s], gpre_ref[...], mod_ref[0, 0:1, :], mod_ref[0, 1:2, :]).astype(BF16)
        pk = _dot(h, w_ref[:, :KV_WIDTH])
        pv = _dot(h, w_ref[:, KV_WIDTH:])
        for e in range(N_KV_HEADS):
            sl = slice(e * HEAD_DIM, (e + 1) * HEAD_DIM)
            k_ref[s, e] = _rms(pk[:, sl], gk_ref[...]).astype(BF16)
            v_ref[s, e] = pv[:, sl].astype(BF16)


def _ctx_kv(ctx, mod_ctx, g_pre, w_kv, g_k, *, samples):
    b, n, _ = ctx.shape
    kv_shape = jax.ShapeDtypeStruct((b, N_KV_HEADS, n, HEAD_DIM), BF16)
    kv_spec = pl.BlockSpec((samples, N_KV_HEADS, n, HEAD_DIM), lambda i: (i, 0, 0, 0))
    return pl.pallas_call(
        _ctx_kv_kernel,
        grid=(b // samples,),
        in_specs=[pl.BlockSpec((samples, n, D_MODEL), lambda i: (i, 0, 0)),
                  pl.BlockSpec((1, N_MOD, D_MODEL), lambda i: (0, 0, 0)),
                  pl.BlockSpec((1, D_MODEL), lambda i: (0, 0)),
                  pl.BlockSpec((D_MODEL, 2 * KV_WIDTH), lambda i: (0, 0)),
                  pl.BlockSpec((1, HEAD_DIM), lambda i: (0, 0))],
        out_specs=[kv_spec, kv_spec],
        out_shape=[kv_shape, kv_shape],
        compiler_params=_params("parallel"),
        name="ctx_kv",
    )(ctx, mod_ctx, g_pre, w_kv, g_k)


def _pool_fold_kernel(wg_ref, ps_ref, wu_ref, o_ref):
    for g in range(len(POOL_WINDOWS)):
        sl = slice(g * POOL_GROUP_WIDTH, (g + 1) * POOL_GROUP_WIDTH)
        o_ref[sl, :] = jnp.dot(wg_ref[g] * ps_ref[:, sl], wu_ref[sl, :], precision=lax.Precision.HIGHEST,
                               preferred_element_type=F32).astype(BF16)


def _pool_fold(w_grp, pool_scale, w_pool_up):
    return pl.pallas_call(
        _pool_fold_kernel,
        out_shape=jax.ShapeDtypeStruct(w_pool_up.shape, BF16),
        compiler_params=pltpu.CompilerParams(vmem_limit_bytes=VMEM_LIMIT_BYTES),
        name="pool_fold",
    )(w_grp, pool_scale, w_pool_up)


def _rope_fold_kernel(cos_ref, sin_ref, gq_ref, gk_ref, cq_ref, sq_ref, ck_ref, sk_ref):
    lane = lax.broadcasted_iota(jnp.int32, (F32_SUBLANES, HEAD_DIM), 1)
    low_half = (lane & ROPE_PARTNER) == 0
    cos = cos_ref[...]
    sin = sin_ref[...]
    for g_ref, c_ref, s_ref, logit_scale in ((gq_ref, cq_ref, sq_ref, Q_SCALE), (gk_ref, ck_ref, sk_ref, 1.0)):
        g = jnp.broadcast_to(g_ref[...], (F32_SUBLANES, HEAD_DIM)) * (logit_scale * math.sqrt(HEAD_DIM))
        g_partner = jnp.where(low_half, pltpu.roll(g, HEAD_DIM - ROPE_PARTNER, 1),
                              pltpu.roll(g, ROPE_PARTNER, 1))
        c_ref[...] = cos * g[0:1, :]
        s_ref[...] = sin * g_partner[0:1, :]


def _rope_fold(cos_t, sin_t, g_q, g_k):
    tab = jax.ShapeDtypeStruct(cos_t.shape, F32)
    return pl.pallas_call(
        _rope_fold_kernel,
        out_shape=[tab] * 4,
        compiler_params=pltpu.CompilerParams(vmem_limit_bytes=VMEM_LIMIT_BYTES),
        name="rope_fold",
    )(cos_t, sin_t, g_q, g_k)


def _in_proj_kernel(x_ref, xp_ref, xn_ref, mod_ref, gpre_ref, w_ref, bg_ref,
                    cq_ref, sq_ref, ck_ref, sk_ref, q_ref, k_ref, v_ref, pool_ref, gate_ref,
                    *, tm, sub, n_tiles, seq):
    i = pl.program_id(1)
    gpre = gpre_ref[...]
    shift = mod_ref[0, 0:1, :]
    scale = mod_ref[0, 1:2, :]
    subs = [slice(s * sub, (s + 1) * sub) for s in range(tm // sub)]
    hb = [_norm_mod(x_ref[0, rs, :], gpre, shift, scale).astype(BF16) for rs in subs]

    w_pool = w_ref[:, POOL_OFF:POOL_OFF + POOL_WIDTH]
    lane = lax.broadcasted_iota(jnp.int32, (sub, HEAD_DIM), 1)
    low_half = (lane & ROPE_PARTNER) == 0
    heads_per_dot = MXU_WIDTH // HEAD_DIM
    gate_chunk = GATE_WIDTH // (ATTN_WIDTH // MXU_WIDTH)

    def norm_rope(t, cos_g, sin_g):
        inv = lax.rsqrt(jnp.sum(t * t, axis=-1, keepdims=True) + HEAD_DIM * EPS)
        partner = jnp.where(low_half, pltpu.roll(t, HEAD_DIM - ROPE_PARTNER, 1),
                            pltpu.roll(t, ROPE_PARTNER, 1))
        return ((t * cos_g + partner * sin_g) * inv).astype(BF16)

    def pool_windows(u_sub):
        u_main = jnp.concatenate(u_sub, axis=0)
        x_halo = jnp.concatenate([xp_ref[0], xn_ref[0]], axis=0)
        u_halo = _dot(_norm_mod(x_halo, gpre, shift, scale).astype(BF16), w_pool)
        u_prev = jnp.where(i > 0, u_halo[:POOL_HALO], 0.0)
        u_next = jnp.where(i < n_tiles - 1, u_halo[POOL_HALO:], 0.0)
        u_ext = jnp.concatenate([u_prev, u_main, u_next], axis=0)
        n_ext = tm + 2 * POOL_HALO
        edge_iota = lax.broadcasted_iota(jnp.int32, (POOL_HALO, POOL_GROUP_WIDTH), 0)
        edge_t = (i * tm + edge_iota, i * tm + (tm - POOL_HALO) + edge_iota)
        for gi, win in enumerate(POOL_WINDOWS):
            sl = slice(gi * POOL_GROUP_WIDTH, (gi + 1) * POOL_GROUP_WIDTH)
            half = win // 2
            run = u_ext[:, sl]
            length = 1
            while length < half:
                run = run + pltpu.roll(run, n_ext - length, 0)
                length *= 2
            window = run + pltpu.roll(run, half, 0)
            window = window[POOL_HALO:POOL_HALO + tm]
            inv_cnt = [1.0 / (jnp.minimum(t + half, seq) - jnp.maximum(t - half, 0)).astype(F32)
                       for t in edge_t]
            inv_cnt = jnp.concatenate(
                [inv_cnt[0], jnp.full((tm - 2 * POOL_HALO, POOL_GROUP_WIDTH), 1.0 / win, F32), inv_cnt[1]],
                axis=0)
            pool_ref[0, :, sl] = (window * inv_cnt - u_main[:, sl]).astype(BF16)

    u_sub = []
    for rs, h in zip(subs, hb):
        u_sub.append(_dot(h, w_pool))
        cq, sq = cq_ref[rs, :], sq_ref[rs, :]
        for j in range(ATTN_WIDTH // MXU_WIDTH):
            pq = _dot(h, w_ref[:, j * MXU_WIDTH:(j + 1) * MXU_WIDTH])
            for e in range(heads_per_dot):
                t = pq[:, e * HEAD_DIM:(e + 1) * HEAD_DIM]
                q_ref[0, j * heads_per_dot + e, rs, :] = norm_rope(t, cq, sq)
        pk = _dot(h, w_ref[:, K_OFF:K_OFF + KV_WIDTH])
        pv = _dot(h, w_ref[:, V_OFF:V_OFF + KV_WIDTH])
        for e in range(N_KV_HEADS):
            sl = slice(e * HEAD_DIM, (e + 1) * HEAD_DIM)
            k_ref[0, e, rs, :] = norm_rope(pk[:, sl], ck_ref[rs, :], sk_ref[rs, :])
            v_ref[0, e, rs, :] = pv[:, sl].astype(BF16)
    pool_windows(u_sub)
    for rs, h in zip(subs, hb):
        for j in range(GATE_WIDTH // gate_chunk):
            sl = slice(j * gate_chunk, (j + 1) * gate_chunk)
            logits = _dot(h, w_ref[:, GATE_OFF + j * gate_chunk:GATE_OFF + (j + 1) * gate_chunk])
            gate_ref[0, rs, sl] = jax.nn.sigmoid(logits + bg_ref[:, sl]).astype(BF16)


def _in_proj(x, mod, g_pre, w_in, b_gate, rope_tabs, *, tm, sub):
    b, s, _ = x.shape
    n_tiles = s // tm
    halo_blocks_per_tile = tm // POOL_HALO
    last_halo_block = s // POOL_HALO - 1
    const2 = lambda bi, i: (0, 0)
    in_specs = [
        pl.BlockSpec((1, tm, D_MODEL), lambda bi, i: (bi, i, 0)),
        pl.BlockSpec((1, POOL_HALO, D_MODEL),
                     lambda bi, i: (bi, jnp.maximum(i * halo_blocks_per_tile - 1, 0), 0)),
        pl.BlockSpec((1, POOL_HALO, D_MODEL),
                     lambda bi, i: (bi, jnp.minimum((i + 1) * halo_blocks_per_tile, last_halo_block), 0)),
        pl.BlockSpec((1, N_MOD, D_MODEL), lambda bi, i: (bi, 0, 0)),
        pl.BlockSpec((1, D_MODEL), const2),
        _VMEM_RESIDENT,
        pl.BlockSpec((1, GATE_WIDTH), const2),
    ] + [pl.BlockSpec((tm, HEAD_DIM), lambda bi, i: (i, 0))] * len(rope_tabs)
    out_specs = [
        pl.BlockSpec((1, N_Q_HEADS, tm, HEAD_DIM), lambda bi, i: (bi, 0, i, 0)),
        pl.BlockSpec((1, N_KV_HEADS, tm, HEAD_DIM), lambda bi, i: (bi, 0, i, 0)),
        pl.BlockSpec((1, N_KV_HEADS, tm, HEAD_DIM), lambda bi, i: (bi, 0, i, 0)),
        pl.BlockSpec((1, tm, POOL_WIDTH), lambda bi, i: (bi, i, 0)),
        pl.BlockSpec((1, tm, GATE_WIDTH), lambda bi, i: (bi, i, 0)),
    ]
    out_shape = [
        jax.ShapeDtypeStruct((b, N_Q_HEADS, s, HEAD_DIM), BF16),
        jax.ShapeDtypeStruct((b, N_KV_HEADS, s, HEAD_DIM), BF16),
        jax.ShapeDtypeStruct((b, N_KV_HEADS, s, HEAD_DIM), BF16),
        jax.ShapeDtypeStruct((b, s, POOL_WIDTH), BF16),
        jax.ShapeDtypeStruct((b, s, GATE_WIDTH), BF16),
    ]
    return pl.pallas_call(
        functools.partial(_in_proj_kernel, tm=tm, sub=sub, n_tiles=n_tiles, seq=s),
        grid=(b, n_tiles),
        in_specs=in_specs,
        out_specs=out_specs,
        out_shape=out_shape,
        compiler_params=_params("parallel", "parallel"),
        name="in_proj",
    )(x, x, x, mod, g_pre, w_in, b_gate, *rope_tabs)


def _attention_kernel(q_ref, kc_ref, vc_ref, kl_ref, vl_ref, o_ref, *, tq, rows):
    kc = kc_ref[0, 0]
    kl = kl_ref[0, 0]
    ones_cols = lambda n: jnp.ones((n, MXU_WIDTH - HEAD_DIM), BF16)
    vc = jnp.concatenate([vc_ref[0, 0], ones_cols(kc.shape[0])], axis=-1)
    vl = jnp.concatenate([vl_ref[0, 0], ones_cols(kl.shape[0])], axis=-1)
    for h in range(Q_GROUP):
        for r in range(tq // rows):
            rs = slice(r * rows, (r + 1) * rows)
            q = q_ref[0, 0, h, rs, :]
            s_c = _dot_nt(q, kc)
            s_l = _dot_nt(q, kl)
            m = jnp.maximum(jnp.max(s_c, axis=-1, keepdims=True),
                            jnp.max(s_l, axis=-1, keepdims=True))
            p_c = jnp.exp2(s_c - m).astype(BF16)
            p_l = jnp.exp2(s_l - m).astype(BF16)
            o = _dot(p_c, vc) + _dot(p_l, vl)
            o_ref[0, rs, h * HEAD_DIM:(h + 1) * HEAD_DIM] = (
                o[:, :HEAD_DIM] / o[:, HEAD_DIM:]).astype(BF16)


def _attention(q, k_ctx, v_ctx, k_lat, v_lat, *, tq, rows):
    b, _, s, _ = q.shape
    n_ctx = k_ctx.shape[2]
    qg = q.reshape(b, N_KV_HEADS, Q_GROUP, s, HEAD_DIM)
    kv_idx = lambda bi, kh, i: (bi, kh, 0, 0)
    return pl.pallas_call(
        functools.partial(_attention_kernel, tq=tq, rows=rows),
        grid=(b, N_KV_HEADS, s // tq),
        in_specs=[pl.BlockSpec((1, 1, Q_GROUP, tq, HEAD_DIM), lambda bi, kh, i: (bi, kh, 0, i, 0)),
                  pl.BlockSpec((1, 1, n_ctx, HEAD_DIM), kv_idx),
                  pl.BlockSpec((1, 1, n_ctx, HEAD_DIM), kv_idx),
                  pl.BlockSpec((1, 1, s, HEAD_DIM), kv_idx),
                  pl.BlockSpec((1, 1, s, HEAD_DIM), kv_idx)],
        out_specs=pl.BlockSpec((1, tq, Q_GROUP * HEAD_DIM), lambda bi, kh, i: (bi, i, kh)),
        out_shape=jax.ShapeDtypeStruct((b, s, ATTN_WIDTH), BF16),
        compiler_params=_params("parallel", "parallel", "parallel"),
        name="attention",
    )(qg, k_ctx, v_ctx, k_lat, v_lat)


def _merge_mlp_kernel(x_ref, a_ref, p_ref, g_ref, mod_ref, gmix_ref, gpre_ref, gpost_ref,
                      wa_ref, wp_ref, wo_ref, w1_ref, w2_ref, o_ref, *, tm, sub, ff_chunk):
    subs = [slice(s * sub, (s + 1) * sub) for s in range(tm // sub)]
    x_mid = []
    for rs in subs:
        ya = _dot(a_ref[0, rs, :], wa_ref[...])
        yp = _dot(p_ref[0, rs, :], wp_ref[...])
        ga = g_ref[0, rs, :D_MODEL].astype(F32)
        gp = g_ref[0, rs, D_MODEL:].astype(F32)
        y = _dot((ga * ya + gp * yp).astype(BF16), wo_ref[...])
        x_mid.append(x_ref[0, rs, :] + mod_ref[0, 2:3, :] * _rms(y, gmix_ref[...]))
    for rs, xs in zip(subs, x_mid):
        h = _norm_mod(xs, gpre_ref[...], mod_ref[0, 3:4, :], mod_ref[0, 4:5, :]).astype(BF16)
        acc = None
        for j in range(D_FF // ff_chunk):
            sl = slice(j * ff_chunk, (j + 1) * ff_chunk)
            t = jnp.maximum(_dot(h, w1_ref[:, sl]), 0.0)
            part = _dot((t * t).astype(BF16), w2_ref[sl, :])
            acc = part if acc is None else acc + part
        o_ref[0, rs, :] = xs + mod_ref[0, 5:6, :] * _rms(acc, gpost_ref[...])


def _merge_mlp(x, attn_o, pooled, gates, mod, g_post_mix, g_pre_mlp, g_post_mlp,
               w_attn_up, w_pool_fold, w_out, w_ff1, w_ff2, *, tm, sub, ff_chunk):
    b, s, _ = x.shape
    tile = lambda width: pl.BlockSpec((1, tm, width), lambda bi, i: (bi, i, 0))
    const2 = lambda bi, i: (0, 0)
    gain = pl.BlockSpec((1, D_MODEL), const2)
    return pl.pallas_call(
        functools.partial(_merge_mlp_kernel, tm=tm, sub=sub, ff_chunk=ff_chunk),
        grid=(b, s // tm),
        in_specs=[tile(D_MODEL), tile(ATTN_WIDTH), tile(POOL_WIDTH), tile(GATE_WIDTH),
                  pl.BlockSpec((1, N_MOD, D_MODEL), lambda bi, i: (bi, 0, 0)),
                  gain, gain, gain] + [_VMEM_RESIDENT] * 5,
        out_specs=tile(D_MODEL),
        out_shape=jax.ShapeDtypeStruct(x.shape, F32),
        compiler_params=_params("parallel", "parallel"),
        name="merge_mlp",
    )(x, attn_o, pooled, gates, mod, g_post_mix, g_pre_mlp, g_post_mlp,
      w_attn_up, w_pool_fold, w_out, w_ff1, w_ff2)


def _rope_tables(seq):
    t = jnp.arange(seq)
    rows = (t // GRID_W).astype(F32)
    cols = (t % GRID_W).astype(F32)
    n_freq = HEAD_DIM // 4
    freqs = ROPE_THETA ** (-jnp.arange(n_freq, dtype=F32) / n_freq)
    ang_row = rows[:, None] * freqs
    ang_col = cols[:, None] * freqs
    cos_r, sin_r, cos_c, sin_c = jnp.cos(ang_row), jnp.sin(ang_row), jnp.cos(ang_col), jnp.sin(ang_col)
    cos_t = jnp.concatenate([cos_r, cos_r, cos_c, cos_c], axis=-1)
    sin_t = jnp.concatenate([-sin_r, sin_r, -sin_c, sin_c], axis=-1)
    return cos_t, sin_t


def kernel(x, c, ctx, c_ctx, w_mod, b_mod, g_pre_mix, g_post_mix, g_pre_mlp, g_post_mlp, w_in, b_gate, g_q, g_k, w_attn_up, w_pool_grp, pool_scale, w_pool_up, w_out, w_ff1, w_ff2):
    depth = w_mod.shape[0]
    assert depth == 1, "single-layer block"
    b, s, _ = x.shape
    row = lambda a: a[0].reshape(1, -1)

    n_cond = -(-(b + 1) // F32_SUBLANES) * F32_SUBLANES
    cond = jnp.concatenate([c, c_ctx[None, :], jnp.zeros((n_cond - b - 1, D_MODEL), F32)], axis=0)
    mod_all = _adaln(cond, w_mod[0], b_mod[0])
    mod = mod_all[:b].reshape(b, N_MOD, D_MODEL)
    mod_ctx = mod_all[b:b + 1].reshape(1, N_MOD, D_MODEL)

    w_in_b = w_in[0].astype(BF16)
    cos_t, sin_t = _rope_tables(s)

    k_ctx, v_ctx = _ctx_kv(ctx, mod_ctx, row(g_pre_mix), w_in_b[:, K_OFF:POOL_OFF], row(g_k), samples=4)
    rope_tabs = _rope_fold(cos_t, sin_t, row(g_q), row(g_k))
    q, k_lat, v_lat, pooled, gates = _in_proj(
        x, mod, row(g_pre_mix), w_in_b, row(b_gate), rope_tabs, tm=1024, sub=256)
    attn_o = _attention(q, k_ctx, v_ctx, k_lat, v_lat, tq=1024, rows=128)
    w_pool_fold = _pool_fold(w_pool_grp[0], row(pool_scale), w_pool_up[0])
    return _merge_mlp(x, attn_o, pooled, gates, mod, row(g_post_mix), row(g_pre_mlp), row(g_post_mlp),
                      w_attn_up[0].astype(BF16), w_pool_fold, w_out[0].astype(BF16),
                      w_ff1[0].astype(BF16), w_ff2[0].astype(BF16), tm=512, sub=256, ff_chunk=1024)
```

```python
import functools
import math

import jax
import jax.numpy as jnp
from jax import lax
from jax.experimental import pallas as pl
from jax.experimental.pallas import tpu as pltpu

D_MODEL = 1024
GRID_W = 64
HEAD_DIM = 128
N_Q_HEADS = 8
N_KV_HEADS = 2
Q_GROUP = N_Q_HEADS // N_KV_HEADS
ATTN_WIDTH = N_Q_HEADS * HEAD_DIM
KV_WIDTH = N_KV_HEADS * HEAD_DIM
POOL_WINDOWS = (2, 4, 8, 16)
POOL_WIDTH = 512
POOL_GROUP_WIDTH = POOL_WIDTH // len(POOL_WINDOWS)
GATE_WIDTH = 2 * D_MODEL
K_OFF = ATTN_WIDTH
V_OFF = K_OFF + KV_WIDTH
POOL_OFF = V_OFF + KV_WIDTH
GATE_OFF = POOL_OFF + POOL_WIDTH
D_FF = 4 * D_MODEL
ROPE_THETA = 10000.0
ROPE_FREQS = HEAD_DIM // 4
ROPE_PARTNER = HEAD_DIM // 2
EPS = 1e-6
N_MOD = 6

F32_SUBLANES = 8
MXU_WIDTH = 256
VMEM_LIMIT_BYTES = 56 * 1024 * 1024

POOL_HALO = max(POOL_WINDOWS) // 2
assert POOL_HALO == F32_SUBLANES

Q_SCALE = (HEAD_DIM ** -0.5) * math.log2(math.e)

BF16 = jnp.bfloat16
F32 = jnp.float32


def _dot(a, b):
    return jnp.dot(a, b, preferred_element_type=F32)


def _dot_nt(a, b):
    return lax.dot_general(a, b, (((1,), (1,)), ((), ())), preferred_element_type=F32)


def _rms(x, gain):
    ms = jnp.mean(x * x, axis=-1, keepdims=True)
    return x * lax.rsqrt(ms + EPS) * gain


def _norm_mod(x, gain, shift, scale):
    return _rms(x, gain * (1.0 + scale)) + shift


def _permute_head_dim(a):
    lead = a.shape[:-1]
    blocks = a.reshape(*lead, -1, 4, ROPE_FREQS)
    return jnp.take(blocks, jnp.array((0, 2, 1, 3)), axis=-2).reshape(a.shape)


_VMEM_RESIDENT = pl.BlockSpec(memory_space=pltpu.VMEM)


def _params(*sem):
    return pltpu.CompilerParams(dimension_semantics=sem, vmem_limit_bytes=VMEM_LIMIT_BYTES)


def _adaln_kernel(c_ref, w_ref, b_ref, o_ref):
    c = c_ref[...]
    a = (c * jax.nn.sigmoid(c)).astype(BF16)
    o_ref[...] = _dot(a, w_ref[...].astype(BF16)) + b_ref[...]


def _adaln(cond, w_mod, b_mod):
    rows = cond.shape[0]
    n_out = w_mod.shape[1]
    bn = D_MODEL
    return pl.pallas_call(
        _adaln_kernel,
        grid=(n_out // bn,),
        in_specs=[pl.BlockSpec((rows, D_MODEL), lambda j: (0, 0)),
                  pl.BlockSpec((D_MODEL, bn), lambda j: (0, j)),
                  pl.BlockSpec((1, bn), lambda j: (0, j))],
        out_specs=pl.BlockSpec((rows, bn), lambda j: (0, j)),
        out_shape=jax.ShapeDtypeStruct((rows, n_out), F32),
        compiler_params=_params("parallel"),
        name="adaln",
    )(cond, w_mod, b_mod.reshape(1, n_out))


def _ctx_kv_kernel(x_ref, mod_ref, gpre_ref, w_ref, gk_ref, k_ref, v_ref):
    for s in range(x_ref.shape[0]):
        h = _norm_mod(x_ref[s], gpre_ref[...], mod_ref[0, 0:1, :], mod_ref[0, 1:2, :]).astype(BF16)
        pk = _dot(h, w_ref[:, :KV_WIDTH])
        pv = _dot(h, w_ref[:, KV_WIDTH:])
        for e in range(N_KV_HEADS):
            sl = slice(e * HEAD_DIM, (e + 1) * HEAD_DIM)
            k_ref[s, e] = _rms(pk[:, sl], gk_ref[...]).astype(BF16)
            v_ref[s, e] = pv[:, sl].astype(BF16)


def _ctx_kv(ctx, mod_ctx, g_pre, w_kv, g_k, *, samples):
    b, n, _ = ctx.shape
    kv_shape = jax.ShapeDtypeStruct((b, N_KV_HEADS, n, HEAD_DIM), BF16)
    kv_spec = pl.BlockSpec((samples, N_KV_HEADS, n, HEAD_DIM), lambda i: (i, 0, 0, 0))
    return pl.pallas_call(
        _ctx_kv_kernel,
        grid=(b // samples,),
        in_specs=[pl.BlockSpec((samples, n, D_MODEL), lambda i: (i, 0, 0)),
                  pl.BlockSpec((1, N_MOD, D_MODEL), lambda i: (0, 0, 0)),
                  pl.BlockSpec((1, D_MODEL), lambda i: (0, 0)),
                  pl.BlockSpec((D_MODEL, 2 * KV_WIDTH), lambda i: (0, 0)),
                  pl.BlockSpec((1, HEAD_DIM), lambda i: (0, 0))],
        out_specs=[kv_spec, kv_spec],
        out_shape=[kv_shape, kv_shape],
        compiler_params=_params("parallel"),
        name="ctx_kv",
    )(ctx, mod_ctx, g_pre, w_kv, g_k)


def _pool_fold_kernel(wg_ref, ps_ref, wu_ref, o_ref):
    for g in range(len(POOL_WINDOWS)):
        sl = slice(g * POOL_GROUP_WIDTH, (g + 1) * POOL_GROUP_WIDTH)
        o_ref[sl, :] = jnp.dot(wg_ref[g] * ps_ref[:, sl], wu_ref[sl, :], precision=lax.Precision.HIGHEST,
                               preferred_element_type=F32).astype(BF16)


def _pool_fold(w_grp, pool_scale, w_pool_up):
    return pl.pallas_call(
        _pool_fold_kernel,
        out_shape=jax.ShapeDtypeStruct(w_pool_up.shape, BF16),
        compiler_params=pltpu.CompilerParams(vmem_limit_bytes=VMEM_LIMIT_BYTES),
        name="pool_fold",
    )(w_grp, pool_scale, w_pool_up)


def _rope_fold_kernel(cos_ref, sin_ref, gq_ref, gk_ref, cq_ref, sq_ref, ck_ref, sk_ref):
    cos = cos_ref[...]
    sin = sin_ref[...]
    for g_ref, c_ref, s_ref, logit_scale in ((gq_ref, cq_ref, sq_ref, Q_SCALE), (gk_ref, ck_ref, sk_ref, 1.0)):
        g = jnp.broadcast_to(g_ref[...], (F32_SUBLANES, HEAD_DIM)) * (logit_scale * math.sqrt(HEAD_DIM))
        g_partner = pltpu.roll(g, ROPE_PARTNER, 1)
        c_ref[...] = cos * g[0:1, :]
        s_ref[...] = sin * g_partner[0:1, :]


def _rope_fold(cos_t, sin_t, g_q, g_k):
    tab = jax.ShapeDtypeStruct(cos_t.shape, F32)
    return pl.pallas_call(
        _rope_fold_kernel,
        out_shape=[tab] * 4,
        compiler_params=pltpu.CompilerParams(vmem_limit_bytes=VMEM_LIMIT_BYTES),
        name="rope_fold",
    )(cos_t, sin_t, g_q, g_k)


def _in_proj_kernel(x_ref, xp_ref, xn_ref, mod_ref, gpre_ref, w_ref, bg_ref,
                    cq_ref, sq_ref, ck_ref, sk_ref, q_ref, k_ref, v_ref, pool_ref, gate_ref,
                    *, tm, sub, n_tiles, seq):
    i = pl.program_id(1)
    gpre = gpre_ref[...]
    shift = mod_ref[0, 0:1, :]
    scale = mod_ref[0, 1:2, :]
    subs = [slice(s * sub, (s + 1) * sub) for s in range(tm // sub)]
    hb = [_norm_mod(x_ref[0, rs, :], gpre, shift, scale).astype(BF16) for rs in subs]

    w_pool = w_ref[:, POOL_OFF:POOL_OFF + POOL_WIDTH]
    heads_per_dot = MXU_WIDTH // HEAD_DIM
    gate_chunk = GATE_WIDTH // (ATTN_WIDTH // MXU_WIDTH)

    def norm_rope(t, cos_g, sin_g):
        inv = lax.rsqrt(jnp.sum(t * t, axis=-1, keepdims=True) + HEAD_DIM * EPS)
        return ((t * cos_g + pltpu.roll(t, ROPE_PARTNER, 1) * sin_g) * inv).astype(BF16)

    def pool_windows(u_sub):
        u_main = jnp.concatenate(u_sub, axis=0)
        x_halo = jnp.concatenate([xp_ref[0], xn_ref[0]], axis=0)
        u_halo = _dot(_norm_mod(x_halo, gpre, shift, scale).astype(BF16), w_pool)
        u_prev = jnp.where(i > 0, u_halo[:POOL_HALO], 0.0)
        u_next = jnp.where(i < n_tiles - 1, u_halo[POOL_HALO:], 0.0)
        u_ext = jnp.concatenate([u_prev, u_main, u_next], axis=0)
        n_ext = tm + 2 * POOL_HALO
        edge_iota = lax.broadcasted_iota(jnp.int32, (POOL_HALO, POOL_GROUP_WIDTH), 0)
        edge_t = (i * tm + edge_iota, i * tm + (tm - POOL_HALO) + edge_iota)
        for gi, win in enumerate(POOL_WINDOWS):
            sl = slice(gi * POOL_GROUP_WIDTH, (gi + 1) * POOL_GROUP_WIDTH)
            half = win // 2
            run = u_ext[:, sl]
            length = 1
            while length < half:
                run = run + pltpu.roll(run, n_ext - length, 0)
                length *= 2
            window = run + pltpu.roll(run, half, 0)
            window = window[POOL_HALO:POOL_HALO + tm]
            inv_cnt = [1.0 / (jnp.minimum(t + half, seq) - jnp.maximum(t - half, 0)).astype(F32)
                       for t in edge_t]
            inv_cnt = jnp.concatenate(
                [inv_cnt[0], jnp.full((tm - 2 * POOL_HALO, POOL_GROUP_WIDTH), 1.0 / win, F32), inv_cnt[1]],
                axis=0)
            pool_ref[0, :, sl] = (window * inv_cnt - u_main[:, sl]).astype(BF16)

    u_sub = []
    for si, (rs, h) in enumerate(zip(subs, hb)):
        u_sub.append(_dot(h, w_pool))
        ts = pl.ds(pl.multiple_of(i * tm + si * sub, sub), sub)
        cq, sq = cq_ref[ts, :], sq_ref[ts, :]
        for j in range(ATTN_WIDTH // MXU_WIDTH):
            pq = _dot(h, w_ref[:, j * MXU_WIDTH:(j + 1) * MXU_WIDTH])
            for e in range(heads_per_dot):
                t = pq[:, e * HEAD_DIM:(e + 1) * HEAD_DIM]
                q_ref[0, j * heads_per_dot + e, rs, :] = norm_rope(t, cq, sq)
        pk = _dot(h, w_ref[:, K_OFF:K_OFF + KV_WIDTH])
        pv = _dot(h, w_ref[:, V_OFF:V_OFF + KV_WIDTH])
        for e in range(N_KV_HEADS):
            sl = slice(e * HEAD_DIM, (e + 1) * HEAD_DIM)
            k_ref[0, e, rs, :] = norm_rope(pk[:, sl], ck_ref[ts, :], sk_ref[ts, :])
            v_ref[0, e, rs, :] = pv[:, sl].astype(BF16)
    pool_windows(u_sub)
    for rs, h in zip(subs, hb):
        for j in range(GATE_WIDTH // gate_chunk):
            sl = slice(j * gate_chunk, (j + 1) * gate_chunk)
            logits = _dot(h, w_ref[:, GATE_OFF + j * gate_chunk:GATE_OFF + (j + 1) * gate_chunk])
            gate_ref[0, rs, sl] = jax.nn.sigmoid(logits + bg_ref[:, sl]).astype(BF16)


def _in_proj(x, mod, g_pre, w_in, b_gate, rope_tabs, *, tm, sub):
    b, s, _ = x.shape
    n_tiles = s // tm
    halo_blocks_per_tile = tm // POOL_HALO
    last_halo_block = s // POOL_HALO - 1
    const2 = lambda bi, i: (0, 0)
    in_specs = [
        pl.BlockSpec((1, tm, D_MODEL), lambda bi, i: (bi, i, 0)),
        pl.BlockSpec((1, POOL_HALO, D_MODEL),
                     lambda bi, i: (bi, jnp.maximum(i * halo_blocks_per_tile - 1, 0), 0)),
        pl.BlockSpec((1, POOL_HALO, D_MODEL),
                     lambda bi, i: (bi, jnp.minimum((i + 1) * halo_blocks_per_tile, last_halo_block), 0)),
        pl.BlockSpec((1, N_MOD, D_MODEL), lambda bi, i: (bi, 0, 0)),
        pl.BlockSpec((1, D_MODEL), const2),
        _VMEM_RESIDENT,
        pl.BlockSpec((1, GATE_WIDTH), const2),
    ] + [_VMEM_RESIDENT] * len(rope_tabs)
    out_specs = [
        pl.BlockSpec((1, N_Q_HEADS, tm, HEAD_DIM), lambda bi, i: (bi, 0, i, 0)),
        pl.BlockSpec((1, N_KV_HEADS, tm, HEAD_DIM), lambda bi, i: (bi, 0, i, 0)),
        pl.BlockSpec((1, N_KV_HEADS, tm, HEAD_DIM), lambda bi, i: (bi, 0, i, 0)),
        pl.BlockSpec((1, tm, POOL_WIDTH), lambda bi, i: (bi, i, 0)),
        pl.BlockSpec((1, tm, GATE_WIDTH), lambda bi, i: (bi, i, 0)),
    ]
    out_shape = [
        jax.ShapeDtypeStruct((b, N_Q_HEADS, s, HEAD_DIM), BF16),
        jax.ShapeDtypeStruct((b, N_KV_HEADS, s, HEAD_DIM), BF16),
        jax.ShapeDtypeStruct((b, N_KV_HEADS, s, HEAD_DIM), BF16),
        jax.ShapeDtypeStruct((b, s, POOL_WIDTH), BF16),
        jax.ShapeDtypeStruct((b, s, GATE_WIDTH), BF16),
    ]
    return pl.pallas_call(
        functools.partial(_in_proj_kernel, tm=tm, sub=sub, n_tiles=n_tiles, seq=s),
        grid=(b, n_tiles),
        in_specs=in_specs,
        out_specs=out_specs,
        out_shape=out_shape,
        compiler_params=_params("parallel", "parallel"),
        name="in_proj",
    )(x, x, x, mod, g_pre, w_in, b_gate, *rope_tabs)


def _attention_kernel(q_ref, kc_ref, vc_ref, kl_ref, vl_ref, o_ref, *, tq, rows):
    kc = kc_ref[0, 0]
    kl = kl_ref[0, 0]
    ones_cols = lambda n: jnp.ones((n, MXU_WIDTH - HEAD_DIM), BF16)
    vc = jnp.concatenate([vc_ref[0, 0], ones_cols(kc.shape[0])], axis=-1)
    vl = jnp.concatenate([vl_ref[0, 0], ones_cols(kl.shape[0])], axis=-1)
    for h in range(Q_GROUP):
        for r in range(tq // rows):
            rs = slice(r * rows, (r + 1) * rows)
            q = q_ref[0, 0, h, rs, :]
            s_c = _dot_nt(q, kc)
            s_l = _dot_nt(q, kl)
            m = jnp.maximum(jnp.max(s_c, axis=-1, keepdims=True),
                            jnp.max(s_l, axis=-1, keepdims=True))
            p_c = jnp.exp2(s_c - m).astype(BF16)
            p_l = jnp.exp2(s_l - m).astype(BF16)
            o = _dot(p_c, vc) + _dot(p_l, vl)
            o_ref[0, rs, h * HEAD_DIM:(h + 1) * HEAD_DIM] = (
                o[:, :HEAD_DIM] / o[:, HEAD_DIM:]).astype(BF16)


def _attention(q, k_ctx, v_ctx, k_lat, v_lat, *, tq, rows):
    b, _, s, _ = q.shape
    n_ctx = k_ctx.shape[2]
    qg = q.reshape(b, N_KV_HEADS, Q_GROUP, s, HEAD_DIM)
    kv_idx = lambda bi, kh, i: (bi, kh, 0, 0)
    return pl.pallas_call(
        functools.partial(_attention_kernel, tq=tq, rows=rows),
        grid=(b, N_KV_HEADS, s // tq),
        in_specs=[pl.BlockSpec((1, 1, Q_GROUP, tq, HEAD_DIM), lambda bi, kh, i: (bi, kh, 0, i, 0)),
                  pl.BlockSpec((1, 1, n_ctx, HEAD_DIM), kv_idx),
                  pl.BlockSpec((1, 1, n_ctx, HEAD_DIM), kv_idx),
                  pl.BlockSpec((1, 1, s, HEAD_DIM), kv_idx),
                  pl.BlockSpec((1, 1, s, HEAD_DIM), kv_idx)],
        out_specs=pl.BlockSpec((1, tq, Q_GROUP * HEAD_DIM), lambda bi, kh, i: (bi, i, kh)),
        out_shape=jax.ShapeDtypeStruct((b, s, ATTN_WIDTH), BF16),
        compiler_params=_params("parallel", "parallel", "parallel"),
        name="attention",
    )(qg, k_ctx, v_ctx, k_lat, v_lat)


def _merge_mlp_kernel(x_ref, a_ref, p_ref, g_ref, mod_ref, gmix_ref, gpre_ref, gpost_ref,
                      wa_ref, wp_ref, wo_ref, w1_ref, w2_ref, o_ref, *, tm, sub, ff_chunk):
    subs = [slice(s * sub, (s + 1) * sub) for s in range(tm // sub)]
    x_mid = []
    for rs in subs:
        ya = _dot(a_ref[0, rs, :], wa_ref[...])
        yp = _dot(p_ref[0, rs, :], wp_ref[...])
        ga = g_ref[0, rs, :D_MODEL].astype(F32)
        gp = g_ref[0, rs, D_MODEL:].astype(F32)
        y = _dot((ga * ya + gp * yp).astype(BF16), wo_ref[...])
        x_mid.append(x_ref[0, rs, :] + mod_ref[0, 2:3, :] * _rms(y, gmix_ref[...]))
    for rs, xs in zip(subs, x_mid):
        h = _norm_mod(xs, gpre_ref[...], mod_ref[0, 3:4, :], mod_ref[0, 4:5, :]).astype(BF16)
        acc = None
        for j in range(D_FF // ff_chunk):
            sl = slice(j * ff_chunk, (j + 1) * ff_chunk)
            t = jnp.maximum(_dot(h, w1_ref[:, sl]), 0.0)
            part = _dot((t * t).astype(BF16), w2_ref[sl, :])
            acc = part if acc is None else acc + part
        o_ref[0, rs, :] = xs + mod_ref[0, 5:6, :] * _rms(acc, gpost_ref[...])


def _merge_mlp(x, attn_o, pooled, gates, mod, g_post_mix, g_pre_mlp, g_post_mlp,
               w_attn_up, w_pool_fold, w_out, w_ff1, w_ff2, *, tm, sub, ff_chunk):
    b, s, _ = x.shape
    tile = lambda width: pl.BlockSpec((1, tm, width), lambda bi, i: (bi, i, 0))
    const2 = lambda bi, i: (0, 0)
    gain = pl.BlockSpec((1, D_MODEL), const2)
    return pl.pallas_call(
        functools.partial(_merge_mlp_kernel, tm=tm, sub=sub, ff_chunk=ff_chunk),
        grid=(b, s // tm),
        in_specs=[tile(D_MODEL), tile(ATTN_WIDTH), tile(POOL_WIDTH), tile(GATE_WIDTH),
                  pl.BlockSpec((1, N_MOD, D_MODEL), lambda bi, i: (bi, 0, 0)),
                  gain, gain, gain] + [_VMEM_RESIDENT] * 5,
        out_specs=tile(D_MODEL),
        out_shape=jax.ShapeDtypeStruct(x.shape, F32),
        compiler_params=_params("parallel", "parallel"),
        name="merge_mlp",
    )(x, attn_o, pooled, gates, mod, g_post_mix, g_pre_mlp, g_post_mlp,
      w_attn_up, w_pool_fold, w_out, w_ff1, w_ff2)


def _rope_tables(seq):
    t = jnp.arange(seq)
    rows = (t // GRID_W).astype(F32)
    cols = (t % GRID_W).astype(F32)
    freqs = ROPE_THETA ** (-jnp.arange(ROPE_FREQS, dtype=F32) / ROPE_FREQS)
    ang_row = rows[:, None] * freqs
    ang_col = cols[:, None] * freqs
    cos_r, sin_r, cos_c, sin_c = jnp.cos(ang_row), jnp.sin(ang_row), jnp.cos(ang_col), jnp.sin(ang_col)
    cos_t = jnp.concatenate([cos_r, cos_c, cos_r, cos_c], axis=-1)
    sin_t = jnp.concatenate([-sin_r, -sin_c, sin_r, sin_c], axis=-1)
    return cos_t, sin_t


def kernel(x, c, ctx, c_ctx, w_mod, b_mod, g_pre_mix, g_post_mix, g_pre_mlp, g_post_mlp, w_in, b_gate, g_q, g_k, w_attn_up, w_pool_grp, pool_scale, w_pool_up, w_out, w_ff1, w_ff2):
    depth = w_mod.shape[0]
    assert depth == 1, "single-layer block"
    b, s, _ = x.shape
    row = lambda a: a[0].reshape(1, -1)

    n_cond = -(-(b + 1) // F32_SUBLANES) * F32_SUBLANES
    cond = jnp.concatenate([c, c_ctx[None, :], jnp.zeros((n_cond - b - 1, D_MODEL), F32)], axis=0)
    mod_all = _adaln(cond, w_mod[0], b_mod[0])
    mod = mod_all[:b].reshape(b, N_MOD, D_MODEL)
    mod_ctx = mod_all[b:b + 1].reshape(1, N_MOD, D_MODEL)

    w_in_b = w_in[0].astype(BF16)
    w_in_b = jnp.concatenate([_permute_head_dim(w_in_b[:, :V_OFF]), w_in_b[:, V_OFF:]], axis=-1)
    g_q_p = _permute_head_dim(row(g_q))
    g_k_p = _permute_head_dim(row(g_k))
    cos_t, sin_t = _rope_tables(s)

    k_ctx, v_ctx = _ctx_kv(ctx, mod_ctx, row(g_pre_mix), w_in_b[:, K_OFF:POOL_OFF], g_k_p, samples=4)
    rope_tabs = _rope_fold(cos_t, sin_t, g_q_p, g_k_p)
    q, k_lat, v_lat, pooled, gates = _in_proj(
        x, mod, row(g_pre_mix), w_in_b, row(b_gate), rope_tabs, tm=1024, sub=256)
    attn_o = _attention(q, k_ctx, v_ctx, k_lat, v_lat, tq=1024, rows=128)
    w_pool_fold = _pool_fold(w_pool_grp[0], row(pool_scale), w_pool_up[0])
    return _merge_mlp(x, attn_o, pooled, gates, mod, row(g_post_mix), row(g_pre_mlp), row(g_post_mlp),
                      w_attn_up[0].astype(BF16), w_pool_fold, w_out[0].astype(BF16),
                      w_ff1[0].astype(BF16), w_ff2[0].astype(BF16), tm=512, sub=256, ff_chunk=1024)
```

```python
import functools
import math

import jax
import jax.numpy as jnp
import numpy as np
from jax import lax
from jax.experimental import pallas as pl
from jax.experimental.pallas import tpu as pltpu

D_MODEL = 1024
GRID_W = 64
HEAD_DIM = 128
N_Q_HEADS = 8
N_KV_HEADS = 2
Q_GROUP = N_Q_HEADS // N_KV_HEADS
ATTN_WIDTH = N_Q_HEADS * HEAD_DIM
KV_WIDTH = N_KV_HEADS * HEAD_DIM
POOL_WINDOWS = (2, 4, 8, 16)
POOL_WIDTH = 512
POOL_GROUP_WIDTH = POOL_WIDTH // len(POOL_WINDOWS)
GATE_WIDTH = 2 * D_MODEL
K_OFF = ATTN_WIDTH
V_OFF = K_OFF + KV_WIDTH
POOL_OFF = V_OFF + KV_WIDTH
GATE_OFF = POOL_OFF + POOL_WIDTH
D_FF = 4 * D_MODEL
ROPE_THETA = 10000.0
ROPE_FREQS = HEAD_DIM // 4
ROPE_PARTNER = HEAD_DIM // 2
EPS = 1e-6
N_MOD = 6

F32_SUBLANES = 8
MXU_WIDTH = 256
VMEM_LIMIT_BYTES = 56 * 1024 * 1024

POOL_HALO = max(POOL_WINDOWS) // 2
assert POOL_HALO == F32_SUBLANES

Q_SCALE = (HEAD_DIM ** -0.5) * math.log2(math.e)

BF16 = jnp.bfloat16
F32 = jnp.float32


def _dot(a, b):
    return jnp.dot(a, b, preferred_element_type=F32)


def _dot_nt(a, b):
    return lax.dot_general(a, b, (((1,), (1,)), ((), ())), preferred_element_type=F32)


def _rms(x, gain):
    ms = jnp.mean(x * x, axis=-1, keepdims=True)
    return x * lax.rsqrt(ms + EPS) * gain


def _norm_mod(x, gain, shift, scale):
    return _rms(x, gain * (1.0 + scale)) + shift


def _permute_head_dim(a):
    lead = a.shape[:-1]
    blocks = a.reshape(*lead, -1, 2, 2, ROPE_FREQS)
    return jnp.swapaxes(blocks, -3, -2).reshape(a.shape)


_VMEM_RESIDENT = pl.BlockSpec(memory_space=pltpu.VMEM)


def _params(*sem):
    return pltpu.CompilerParams(dimension_semantics=sem, vmem_limit_bytes=VMEM_LIMIT_BYTES)


def _adaln_kernel(c_ref, w_ref, b_ref, o_ref):
    c = c_ref[...]
    a = (c * jax.nn.sigmoid(c)).astype(BF16)
    o_ref[...] = _dot(a, w_ref[...].astype(BF16)) + b_ref[...]


def _adaln(cond, w_mod, b_mod):
    rows = cond.shape[0]
    n_out = w_mod.shape[1]
    bn = D_MODEL
    return pl.pallas_call(
        _adaln_kernel,
        grid=(n_out // bn,),
        in_specs=[pl.BlockSpec((rows, D_MODEL), lambda j: (0, 0)),
                  pl.BlockSpec((D_MODEL, bn), lambda j: (0, j)),
                  pl.BlockSpec((1, bn), lambda j: (0, j))],
        out_specs=pl.BlockSpec((rows, bn), lambda j: (0, j)),
        out_shape=jax.ShapeDtypeStruct((rows, n_out), F32),
        compiler_params=_params("parallel"),
        name="adaln",
    )(cond, w_mod, b_mod.reshape(1, n_out))


def _ctx_kv_kernel(x_ref, mod_ref, gpre_ref, w_ref, gk_ref, k_ref, v_ref):
    for s in range(x_ref.shape[0]):
        h = _norm_mod(x_ref[s], gpre_ref[...], mod_ref[0, 0:1, :], mod_ref[0, 1:2, :]).astype(BF16)
        pk = _dot(h, w_ref[:, :KV_WIDTH])
        pv = _dot(h, w_ref[:, KV_WIDTH:])
        for e in range(N_KV_HEADS):
            sl = slice(e * HEAD_DIM, (e + 1) * HEAD_DIM)
            k_ref[s, e] = _rms(pk[:, sl], gk_ref[...]).astype(BF16)
            v_ref[s, e] = pv[:, sl].astype(BF16)


def _ctx_kv(ctx, mod_ctx, g_pre, w_kv, g_k, *, samples):
    b, n, _ = ctx.shape
    kv_shape = jax.ShapeDtypeStruct((b, N_KV_HEADS, n, HEAD_DIM), BF16)
    kv_spec = pl.BlockSpec((samples, N_KV_HEADS, n, HEAD_DIM), lambda i: (i, 0, 0, 0))
    return pl.pallas_call(
        _ctx_kv_kernel,
        grid=(b // samples,),
        in_specs=[pl.BlockSpec((samples, n, D_MODEL), lambda i: (i, 0, 0)),
                  pl.BlockSpec((1, N_MOD, D_MODEL), lambda i: (0, 0, 0)),
                  pl.BlockSpec((1, D_MODEL), lambda i: (0, 0)),
                  pl.BlockSpec((D_MODEL, 2 * KV_WIDTH), lambda i: (0, 0)),
                  pl.BlockSpec((1, HEAD_DIM), lambda i: (0, 0))],
        out_specs=[kv_spec, kv_spec],
        out_shape=[kv_shape, kv_shape],
        compiler_params=_params("parallel"),
        name="ctx_kv",
    )(ctx, mod_ctx, g_pre, w_kv, g_k)


def _pool_fold_kernel(wg_ref, ps_ref, wu_ref, o_ref):
    for g in range(len(POOL_WINDOWS)):
        sl = slice(g * POOL_GROUP_WIDTH, (g + 1) * POOL_GROUP_WIDTH)
        o_ref[sl, :] = jnp.dot(wg_ref[g] * ps_ref[:, sl], wu_ref[sl, :], precision=lax.Precision.HIGHEST,
                               preferred_element_type=F32).astype(BF16)


def _pool_fold(w_grp, pool_scale, w_pool_up):
    return pl.pallas_call(
        _pool_fold_kernel,
        out_shape=jax.ShapeDtypeStruct(w_pool_up.shape, BF16),
        compiler_params=pltpu.CompilerParams(vmem_limit_bytes=VMEM_LIMIT_BYTES),
        name="pool_fold",
    )(w_grp, pool_scale, w_pool_up)


def _rope_fold_kernel(cos_ref, sin_ref, gq_ref, gk_ref, cq_ref, sq_ref, ck_ref, sk_ref):
    cos = cos_ref[...]
    sin = sin_ref[...]
    for g_ref, c_ref, s_ref, logit_scale in ((gq_ref, cq_ref, sq_ref, Q_SCALE), (gk_ref, ck_ref, sk_ref, 1.0)):
        g = jnp.broadcast_to(g_ref[...], (F32_SUBLANES, HEAD_DIM)) * (logit_scale * math.sqrt(HEAD_DIM))
        g_partner = pltpu.roll(g, ROPE_PARTNER, 1)
        c_ref[...] = cos * g[0:1, :]
        s_ref[...] = sin * g_partner[0:1, :]


def _rope_fold(cos_t, sin_t, g_q, g_k):
    tab = jax.ShapeDtypeStruct(cos_t.shape, F32)
    return pl.pallas_call(
        _rope_fold_kernel,
        out_shape=[tab] * 4,
        compiler_params=pltpu.CompilerParams(vmem_limit_bytes=VMEM_LIMIT_BYTES),
        name="rope_fold",
    )(cos_t, sin_t, g_q, g_k)


def _in_proj_kernel(x_ref, xp_ref, xn_ref, mod_ref, gpre_ref, w_ref, bg_ref,
                    cq_ref, sq_ref, ck_ref, sk_ref, q_ref, k_ref, v_ref, pool_ref, gate_ref,
                    *, tm, sub, n_tiles, seq):
    i = pl.program_id(1)
    gpre = gpre_ref[...]
    shift = mod_ref[0, 0:1, :]
    scale = mod_ref[0, 1:2, :]
    subs = [slice(s * sub, (s + 1) * sub) for s in range(tm // sub)]
    hb = [_norm_mod(x_ref[0, rs, :], gpre, shift, scale).astype(BF16) for rs in subs]

    w_pool = w_ref[:, POOL_OFF:POOL_OFF + POOL_WIDTH]
    heads_per_dot = MXU_WIDTH // HEAD_DIM
    gate_chunk = GATE_WIDTH // (ATTN_WIDTH // MXU_WIDTH)

    def norm_rope(t, cos_g, sin_g):
        inv = lax.rsqrt(jnp.sum(t * t, axis=-1, keepdims=True) + HEAD_DIM * EPS)
        return ((t * cos_g + pltpu.roll(t, ROPE_PARTNER, 1) * sin_g) * inv).astype(BF16)

    def pool_windows(u_sub):
        u_main = jnp.concatenate(u_sub, axis=0)
        x_halo = jnp.concatenate([xp_ref[0], xn_ref[0]], axis=0)
        u_halo = _dot(_norm_mod(x_halo, gpre, shift, scale).astype(BF16), w_pool)
        u_prev = jnp.where(i > 0, u_halo[:POOL_HALO], 0.0)
        u_next = jnp.where(i < n_tiles - 1, u_halo[POOL_HALO:], 0.0)
        u_ext = jnp.concatenate([u_prev, u_main, u_next], axis=0)
        n_ext = tm + 2 * POOL_HALO
        edge_iota = lax.broadcasted_iota(jnp.int32, (POOL_HALO, POOL_GROUP_WIDTH), 0)
        edge_t = (i * tm + edge_iota, i * tm + (tm - POOL_HALO) + edge_iota)
        for gi, win in enumerate(POOL_WINDOWS):
            sl = slice(gi * POOL_GROUP_WIDTH, (gi + 1) * POOL_GROUP_WIDTH)
            half = win // 2
            run = u_ext[:, sl]
            length = 1
            while length < half:
                run = run + pltpu.roll(run, n_ext - length, 0)
                length *= 2
            window = run + pltpu.roll(run, half, 0)
            window = window[POOL_HALO:POOL_HALO + tm]
            inv_cnt = [1.0 / (jnp.minimum(t + half, seq) - jnp.maximum(t - half, 0)).astype(F32)
                       for t in edge_t]
            inv_cnt = jnp.concatenate(
                [inv_cnt[0], jnp.full((tm - 2 * POOL_HALO, POOL_GROUP_WIDTH), 1.0 / win, F32), inv_cnt[1]],
                axis=0)
            pool_ref[0, :, sl] = (window * inv_cnt - u_main[:, sl]).astype(BF16)

    u_sub = []
    for si, (rs, h) in enumerate(zip(subs, hb)):
        u_sub.append(_dot(h, w_pool))
        ts = pl.ds(pl.multiple_of(i * tm + si * sub, sub), sub)
        cq, sq = cq_ref[ts, :], sq_ref[ts, :]
        for j in range(ATTN_WIDTH // MXU_WIDTH):
            pq = _dot(h, w_ref[:, j * MXU_WIDTH:(j + 1) * MXU_WIDTH])
            for e in range(heads_per_dot):
                t = pq[:, e * HEAD_DIM:(e + 1) * HEAD_DIM]
                q_ref[0, j * heads_per_dot + e, rs, :] = norm_rope(t, cq, sq)
        pk = _dot(h, w_ref[:, K_OFF:K_OFF + KV_WIDTH])
        pv = _dot(h, w_ref[:, V_OFF:V_OFF + KV_WIDTH])
        for e in range(N_KV_HEADS):
            sl = slice(e * HEAD_DIM, (e + 1) * HEAD_DIM)
            k_ref[0, e, rs, :] = norm_rope(pk[:, sl], ck_ref[ts, :], sk_ref[ts, :])
            v_ref[0, e, rs, :] = pv[:, sl].astype(BF16)
    pool_windows(u_sub)
    for rs, h in zip(subs, hb):
        for j in range(GATE_WIDTH // gate_chunk):
            sl = slice(j * gate_chunk, (j + 1) * gate_chunk)
            logits = _dot(h, w_ref[:, GATE_OFF + j * gate_chunk:GATE_OFF + (j + 1) * gate_chunk])
            gate_ref[0, rs, sl] = jax.nn.sigmoid(logits + bg_ref[:, sl]).astype(BF16)


def _in_proj(x, mod, g_pre, w_in, b_gate, rope_tabs, *, tm, sub):
    b, s, _ = x.shape
    n_tiles = s // tm
    halo_blocks_per_tile = tm // POOL_HALO
    last_halo_block = s // POOL_HALO - 1
    const2 = lambda bi, i: (0, 0)
    in_specs = [
        pl.BlockSpec((1, tm, D_MODEL), lambda bi, i: (bi, i, 0)),
        pl.BlockSpec((1, POOL_HALO, D_MODEL),
                     lambda bi, i: (bi, jnp.maximum(i * halo_blocks_per_tile - 1, 0), 0)),
        pl.BlockSpec((1, POOL_HALO, D_MODEL),
                     lambda bi, i: (bi, jnp.minimum((i + 1) * halo_blocks_per_tile, last_halo_block), 0)),
        pl.BlockSpec((1, N_MOD, D_MODEL), lambda bi, i: (bi, 0, 0)),
        pl.BlockSpec((1, D_MODEL), const2),
        _VMEM_RESIDENT,
        pl.BlockSpec((1, GATE_WIDTH), const2),
    ] + [_VMEM_RESIDENT] * len(rope_tabs)
    out_specs = [
        pl.BlockSpec((1, N_Q_HEADS, tm, HEAD_DIM), lambda bi, i: (bi, 0, i, 0)),
        pl.BlockSpec((1, N_KV_HEADS, tm, HEAD_DIM), lambda bi, i: (bi, 0, i, 0)),
        pl.BlockSpec((1, N_KV_HEADS, tm, HEAD_DIM), lambda bi, i: (bi, 0, i, 0)),
        pl.BlockSpec((1, tm, POOL_WIDTH), lambda bi, i: (bi, i, 0)),
        pl.BlockSpec((1, tm, GATE_WIDTH), lambda bi, i: (bi, i, 0)),
    ]
    out_shape = [
        jax.ShapeDtypeStruct((b, N_Q_HEADS, s, HEAD_DIM), BF16),
        jax.ShapeDtypeStruct((b, N_KV_HEADS, s, HEAD_DIM), BF16),
        jax.ShapeDtypeStruct((b, N_KV_HEADS, s, HEAD_DIM), BF16),
        jax.ShapeDtypeStruct((b, s, POOL_WIDTH), BF16),
        jax.ShapeDtypeStruct((b, s, GATE_WIDTH), BF16),
    ]
    return pl.pallas_call(
        functools.partial(_in_proj_kernel, tm=tm, sub=sub, n_tiles=n_tiles, seq=s),
        grid=(b, n_tiles),
        in_specs=in_specs,
        out_specs=out_specs,
        out_shape=out_shape,
        compiler_params=_params("parallel", "parallel"),
        name="in_proj",
    )(x, x, x, mod, g_pre, w_in, b_gate, *rope_tabs)


def _attention_kernel(q_ref, kc_ref, vc_ref, kl_ref, vl_ref, o_ref, *, tq, rows):
    kc = kc_ref[0, 0]
    kl = kl_ref[0, 0]
    ones_cols = lambda n: jnp.ones((n, MXU_WIDTH - HEAD_DIM), BF16)
    vc = jnp.concatenate([vc_ref[0, 0], ones_cols(kc.shape[0])], axis=-1)
    vl = jnp.concatenate([vl_ref[0, 0], ones_cols(kl.shape[0])], axis=-1)
    for h in range(Q_GROUP):
        for r in range(tq // rows):
            rs = slice(r * rows, (r + 1) * rows)
            q = q_ref[0, 0, h, rs, :]
            s_c = _dot_nt(q, kc)
            s_l = _dot_nt(q, kl)
            m = jnp.maximum(jnp.max(s_c, axis=-1, keepdims=True),
                            jnp.max(s_l, axis=-1, keepdims=True))
            p_c = jnp.exp2(s_c - m).astype(BF16)
            p_l = jnp.exp2(s_l - m).astype(BF16)
            o = _dot(p_c, vc) + _dot(p_l, vl)
            o_ref[0, rs, h * HEAD_DIM:(h + 1) * HEAD_DIM] = (
                o[:, :HEAD_DIM] / o[:, HEAD_DIM:]).astype(BF16)


def _attention(q, k_ctx, v_ctx, k_lat, v_lat, *, tq, rows):
    b, _, s, _ = q.shape
    n_ctx = k_ctx.shape[2]
    qg = q.reshape(b, N_KV_HEADS, Q_GROUP, s, HEAD_DIM)
    kv_idx = lambda bi, kh, i: (bi, kh, 0, 0)
    return pl.pallas_call(
        functools.partial(_attention_kernel, tq=tq, rows=rows),
        grid=(b, N_KV_HEADS, s // tq),
        in_specs=[pl.BlockSpec((1, 1, Q_GROUP, tq, HEAD_DIM), lambda bi, kh, i: (bi, kh, 0, i, 0)),
                  pl.BlockSpec((1, 1, n_ctx, HEAD_DIM), kv_idx),
                  pl.BlockSpec((1, 1, n_ctx, HEAD_DIM), kv_idx),
                  pl.BlockSpec((1, 1, s, HEAD_DIM), kv_idx),
                  pl.BlockSpec((1, 1, s, HEAD_DIM), kv_idx)],
        out_specs=pl.BlockSpec((1, tq, Q_GROUP * HEAD_DIM), lambda bi, kh, i: (bi, i, kh)),
        out_shape=jax.ShapeDtypeStruct((b, s, ATTN_WIDTH), BF16),
        compiler_params=_params("parallel", "parallel", "parallel"),
        name="attention",
    )(qg, k_ctx, v_ctx, k_lat, v_lat)


def _merge_mlp_kernel(x_ref, a_ref, p_ref, g_ref, mod_ref, gmix_ref, gpre_ref, gpost_ref,
                      wa_ref, wp_ref, wo_ref, w1_ref, w2_ref, o_ref, *, tm, sub, ff_chunk):
    subs = [slice(s * sub, (s + 1) * sub) for s in range(tm // sub)]
    x_mid = []
    for rs in subs:
        ya = _dot(a_ref[0, rs, :], wa_ref[...])
        yp = _dot(p_ref[0, rs, :], wp_ref[...])
        ga = g_ref[0, rs, :D_MODEL].astype(F32)
        gp = g_ref[0, rs, D_MODEL:].astype(F32)
        y = _dot((ga * ya + gp * yp).astype(BF16), wo_ref[...])
        x_mid.append(x_ref[0, rs, :] + mod_ref[0, 2:3, :] * _rms(y, gmix_ref[...]))
    for rs, xs in zip(subs, x_mid):
        h = _norm_mod(xs, gpre_ref[...], mod_ref[0, 3:4, :], mod_ref[0, 4:5, :]).astype(BF16)
        acc = None
        for j in range(D_FF // ff_chunk):
            sl = slice(j * ff_chunk, (j + 1) * ff_chunk)
            t = jnp.maximum(_dot(h, w1_ref[:, sl]), 0.0)
            part = _dot((t * t).astype(BF16), w2_ref[sl, :])
            acc = part if acc is None else acc + part
        o_ref[0, rs, :] = xs + mod_ref[0, 5:6, :] * _rms(acc, gpost_ref[...])


def _merge_mlp(x, attn_o, pooled, gates, mod, g_post_mix, g_pre_mlp, g_post_mlp,
               w_attn_up, w_pool_fold, w_out, w_ff1, w_ff2, *, tm, sub, ff_chunk):
    b, s, _ = x.shape
    tile = lambda width: pl.BlockSpec((1, tm, width), lambda bi, i: (bi, i, 0))
    const2 = lambda bi, i: (0, 0)
    gain = pl.BlockSpec((1, D_MODEL), const2)
    return pl.pallas_call(
        functools.partial(_merge_mlp_kernel, tm=tm, sub=sub, ff_chunk=ff_chunk),
        grid=(b, s // tm),
        in_specs=[tile(D_MODEL), tile(ATTN_WIDTH), tile(POOL_WIDTH), tile(GATE_WIDTH),
                  pl.BlockSpec((1, N_MOD, D_MODEL), lambda bi, i: (bi, 0, 0)),
                  gain, gain, gain] + [_VMEM_RESIDENT] * 5,
        out_specs=tile(D_MODEL),
        out_shape=jax.ShapeDtypeStruct(x.shape, F32),
        compiler_params=_params("parallel", "parallel"),
        name="merge_mlp",
    )(x, attn_o, pooled, gates, mod, g_post_mix, g_pre_mlp, g_post_mlp,
      w_attn_up, w_pool_fold, w_out, w_ff1, w_ff2)


def _rope_tables(seq):
    t = np.arange(seq)
    rows = (t // GRID_W).astype(np.float64)
    cols = (t % GRID_W).astype(np.float64)
    freqs = ROPE_THETA ** (-np.arange(ROPE_FREQS, dtype=np.float64) / ROPE_FREQS)
    ang_row = rows[:, None] * freqs
    ang_col = cols[:, None] * freqs
    cos_r, sin_r, cos_c, sin_c = np.cos(ang_row), np.sin(ang_row), np.cos(ang_col), np.sin(ang_col)
    cos_t = np.concatenate([cos_r, cos_c, cos_r, cos_c], axis=-1).astype(np.float32)
    sin_t = np.concatenate([-sin_r, -sin_c, sin_r, sin_c], axis=-1).astype(np.float32)
    return jnp.asarray(cos_t), jnp.asarray(sin_t)


def kernel(x, c, ctx, c_ctx, w_mod, b_mod, g_pre_mix, g_post_mix, g_pre_mlp, g_post_mlp, w_in, b_gate, g_q, g_k, w_attn_up, w_pool_grp, pool_scale, w_pool_up, w_out, w_ff1, w_ff2):
    depth = w_mod.shape[0]
    assert depth == 1, "single-layer block"
    b, s, _ = x.shape
    row = lambda a: a[0].reshape(1, -1)

    n_cond = -(-(b + 1) // F32_SUBLANES) * F32_SUBLANES
    cond = jnp.concatenate([c, c_ctx[None, :], jnp.zeros((n_cond - b - 1, D_MODEL), F32)], axis=0)
    mod_all = _adaln(cond, w_mod[0], b_mod[0])
    mod = mod_all[:b].reshape(b, N_MOD, D_MODEL)
    mod_ctx = mod_all[b:b + 1].reshape(1, N_MOD, D_MODEL)

    w_in_b = w_in[0].astype(BF16)
    w_in_b = jnp.concatenate([_permute_head_dim(w_in_b[:, :V_OFF]), w_in_b[:, V_OFF:]], axis=-1)
    g_q_p = _permute_head_dim(row(g_q))
    g_k_p = _permute_head_dim(row(g_k))
    cos_t, sin_t = _rope_tables(s)

    k_ctx, v_ctx = _ctx_kv(ctx, mod_ctx, row(g_pre_mix), w_in_b[:, K_OFF:POOL_OFF], g_k_p, samples=4)
    rope_tabs = _rope_fold(cos_t, sin_t, g_q_p, g_k_p)
    q, k_lat, v_lat, pooled, gates = _in_proj(
        x, mod, row(g_pre_mix), w_in_b, row(b_gate), rope_tabs, tm=1024, sub=256)
    attn_o = _attention(q, k_ctx, v_ctx, k_lat, v_lat, tq=1024, rows=128)
    w_pool_fold = _pool_fold(w_pool_grp[0], row(pool_scale), w_pool_up[0])
    return _merge_mlp(x, attn_o, pooled, gates, mod, row(g_post_mix), row(g_pre_mlp), row(g_post_mlp),
                      w_attn_up[0].astype(BF16), w_pool_fold, w_out[0].astype(BF16),
                      w_ff1[0].astype(BF16), w_ff2[0].astype(BF16), tm=512, sub=256, ff_chunk=1024)
```

```python
import functools
import math

import jax
import jax.numpy as jnp
import numpy as np
from jax import lax
from jax.experimental import pallas as pl
from jax.experimental.pallas import tpu as pltpu

D_MODEL = 1024
GRID_W = 64
HEAD_DIM = 128
N_Q_HEADS = 8
N_KV_HEADS = 2
Q_GROUP = N_Q_HEADS // N_KV_HEADS
ATTN_WIDTH = N_Q_HEADS * HEAD_DIM
KV_WIDTH = N_KV_HEADS * HEAD_DIM
POOL_WINDOWS = (2, 4, 8, 16)
POOL_WIDTH = 512
POOL_GROUP_WIDTH = POOL_WIDTH // len(POOL_WINDOWS)
GATE_WIDTH = 2 * D_MODEL
K_OFF = ATTN_WIDTH
V_OFF = K_OFF + KV_WIDTH
POOL_OFF = V_OFF + KV_WIDTH
GATE_OFF = POOL_OFF + POOL_WIDTH
D_FF = 4 * D_MODEL
ROPE_THETA = 10000.0
ROPE_FREQS = HEAD_DIM // 4
ROPE_PARTNER = HEAD_DIM // 2
EPS = 1e-6
N_MOD = 6

F32_SUBLANES = 8
MXU_WIDTH = 256
VMEM_LIMIT_BYTES = 62 * 1024 * 1024

POOL_HALO = max(POOL_WINDOWS) // 2
assert POOL_HALO == F32_SUBLANES

Q_SCALE = (HEAD_DIM ** -0.5) * math.log2(math.e)

BF16 = jnp.bfloat16
F32 = jnp.float32


def _dot(a, b):
    return jnp.dot(a, b, preferred_element_type=F32)


def _dot_nt(a, b):
    return lax.dot_general(a, b, (((1,), (1,)), ((), ())), preferred_element_type=F32)


def _rms(x, gain):
    ms = jnp.mean(x * x, axis=-1, keepdims=True)
    return x * lax.rsqrt(ms + EPS) * gain


def _norm_mod(x, gain, shift, scale):
    return _rms(x, gain * (1.0 + scale)) + shift


def _permute_head_dim(a):
    lead = a.shape[:-1]
    blocks = a.reshape(*lead, -1, 2, 2, ROPE_FREQS)
    return jnp.swapaxes(blocks, -3, -2).reshape(a.shape)


_VMEM_RESIDENT = pl.BlockSpec(memory_space=pltpu.VMEM)


def _params(*sem):
    return pltpu.CompilerParams(dimension_semantics=sem, vmem_limit_bytes=VMEM_LIMIT_BYTES)


def _adaln_kernel(c_ref, w_ref, b_ref, o_ref):
    c = c_ref[...]
    a = (c * jax.nn.sigmoid(c)).astype(BF16)
    o_ref[...] = _dot(a, w_ref[...].astype(BF16)) + b_ref[...]


def _adaln(cond, w_mod, b_mod):
    rows = cond.shape[0]
    n_out = w_mod.shape[1]
    bn = D_MODEL
    return pl.pallas_call(
        _adaln_kernel,
        grid=(n_out // bn,),
        in_specs=[pl.BlockSpec((rows, D_MODEL), lambda j: (0, 0)),
                  pl.BlockSpec((D_MODEL, bn), lambda j: (0, j)),
                  pl.BlockSpec((1, bn), lambda j: (0, j))],
        out_specs=pl.BlockSpec((rows, bn), lambda j: (0, j)),
        out_shape=jax.ShapeDtypeStruct((rows, n_out), F32),
        compiler_params=_params("parallel"),
        name="adaln",
    )(cond, w_mod, b_mod.reshape(1, n_out))


def _ctx_kv_kernel(x_ref, mod_ref, gpre_ref, w_ref, gk_ref, k_ref, v_ref):
    for s in range(x_ref.shape[0]):
        h = _norm_mod(x_ref[s], gpre_ref[...], mod_ref[0, 0:1, :], mod_ref[0, 1:2, :]).astype(BF16)
        pk = _dot(h, w_ref[:, :KV_WIDTH])
        pv = _dot(h, w_ref[:, KV_WIDTH:])
        for e in range(N_KV_HEADS):
            sl = slice(e * HEAD_DIM, (e + 1) * HEAD_DIM)
            k_ref[s, e] = _rms(pk[:, sl], gk_ref[...]).astype(BF16)
            v_ref[s, e] = pv[:, sl].astype(BF16)


def _ctx_kv(ctx, mod_ctx, g_pre, w_kv, g_k, *, samples):
    b, n, _ = ctx.shape
    kv_shape = jax.ShapeDtypeStruct((b, N_KV_HEADS, n, HEAD_DIM), BF16)
    kv_spec = pl.BlockSpec((samples, N_KV_HEADS, n, HEAD_DIM), lambda i: (i, 0, 0, 0))
    return pl.pallas_call(
        _ctx_kv_kernel,
        grid=(b // samples,),
        in_specs=[pl.BlockSpec((samples, n, D_MODEL), lambda i: (i, 0, 0)),
                  pl.BlockSpec((1, N_MOD, D_MODEL), lambda i: (0, 0, 0)),
                  pl.BlockSpec((1, D_MODEL), lambda i: (0, 0)),
                  pl.BlockSpec((D_MODEL, 2 * KV_WIDTH), lambda i: (0, 0)),
                  pl.BlockSpec((1, HEAD_DIM), lambda i: (0, 0))],
        out_specs=[kv_spec, kv_spec],
        out_shape=[kv_shape, kv_shape],
        compiler_params=_params("parallel"),
        name="ctx_kv",
    )(ctx, mod_ctx, g_pre, w_kv, g_k)


def _pool_fold_kernel(wg_ref, ps_ref, wu_ref, o_ref):
    for g in range(len(POOL_WINDOWS)):
        sl = slice(g * POOL_GROUP_WIDTH, (g + 1) * POOL_GROUP_WIDTH)
        o_ref[sl, :] = jnp.dot(wg_ref[g] * ps_ref[:, sl], wu_ref[sl, :], precision=lax.Precision.HIGHEST,
                               preferred_element_type=F32).astype(BF16)


def _pool_fold(w_grp, pool_scale, w_pool_up):
    return pl.pallas_call(
        _pool_fold_kernel,
        out_shape=jax.ShapeDtypeStruct(w_pool_up.shape, BF16),
        compiler_params=pltpu.CompilerParams(vmem_limit_bytes=VMEM_LIMIT_BYTES),
        name="pool_fold",
    )(w_grp, pool_scale, w_pool_up)


def _rope_fold_kernel(cos_ref, sin_ref, gq_ref, gk_ref, cq_ref, sq_ref, ck_ref, sk_ref):
    cos = cos_ref[...]
    sin = sin_ref[...]
    for g_ref, c_ref, s_ref, logit_scale in ((gq_ref, cq_ref, sq_ref, Q_SCALE), (gk_ref, ck_ref, sk_ref, 1.0)):
        g = jnp.broadcast_to(g_ref[...], (F32_SUBLANES, HEAD_DIM)) * (logit_scale * math.sqrt(HEAD_DIM))
        g_partner = pltpu.roll(g, ROPE_PARTNER, 1)
        c_ref[...] = cos * g[0:1, :]
        s_ref[...] = sin * g_partner[0:1, :]


def _rope_fold(cos_t, sin_t, g_q, g_k):
    tab = jax.ShapeDtypeStruct(cos_t.shape, F32)
    return pl.pallas_call(
        _rope_fold_kernel,
        out_shape=[tab] * 4,
        compiler_params=pltpu.CompilerParams(vmem_limit_bytes=VMEM_LIMIT_BYTES),
        name="rope_fold",
    )(cos_t, sin_t, g_q, g_k)


def _in_proj_kernel(x_ref, xp_ref, xn_ref, mod_ref, gpre_ref, w_ref, bg_ref,
                    cq_ref, sq_ref, ck_ref, sk_ref, q_ref, k_ref, v_ref, pool_ref, gate_ref,
                    *, tm, sub, n_tiles, seq):
    i = pl.program_id(1)
    gpre = gpre_ref[...]
    shift = mod_ref[0, 0:1, :]
    scale = mod_ref[0, 1:2, :]
    subs = [slice(s * sub, (s + 1) * sub) for s in range(tm // sub)]
    hb = [_norm_mod(x_ref[0, rs, :], gpre, shift, scale).astype(BF16) for rs in subs]

    w_pool = w_ref[:, POOL_OFF:POOL_OFF + POOL_WIDTH]
    heads_per_dot = MXU_WIDTH // HEAD_DIM
    gate_chunk = GATE_WIDTH // (ATTN_WIDTH // MXU_WIDTH)

    def norm_rope(t, cos_g, sin_g):
        inv = lax.rsqrt(jnp.sum(t * t, axis=-1, keepdims=True) + HEAD_DIM * EPS)
        return ((t * cos_g + pltpu.roll(t, ROPE_PARTNER, 1) * sin_g) * inv).astype(BF16)

    def pool_windows(u_sub):
        u_main = jnp.concatenate(u_sub, axis=0)
        x_halo = jnp.concatenate([xp_ref[0], xn_ref[0]], axis=0)
        u_halo = _dot(_norm_mod(x_halo, gpre, shift, scale).astype(BF16), w_pool)
        u_prev = jnp.where(i > 0, u_halo[:POOL_HALO], 0.0)
        u_next = jnp.where(i < n_tiles - 1, u_halo[POOL_HALO:], 0.0)
        u_ext = jnp.concatenate([u_prev, u_main, u_next], axis=0)
        n_ext = tm + 2 * POOL_HALO
        edge_iota = lax.broadcasted_iota(jnp.int32, (POOL_HALO, POOL_GROUP_WIDTH), 0)
        edge_t = (i * tm + edge_iota, i * tm + (tm - POOL_HALO) + edge_iota)
        for gi, win in enumerate(POOL_WINDOWS):
            sl = slice(gi * POOL_GROUP_WIDTH, (gi + 1) * POOL_GROUP_WIDTH)
            half = win // 2
            run = u_ext[:, sl]
            length = 1
            while length < half:
                run = run + pltpu.roll(run, n_ext - length, 0)
                length *= 2
            window = run + pltpu.roll(run, half, 0)
            window = window[POOL_HALO:POOL_HALO + tm]
            inv_cnt = [1.0 / (jnp.minimum(t + half, seq) - jnp.maximum(t - half, 0)).astype(F32)
                       for t in edge_t]
            inv_cnt = jnp.concatenate(
                [inv_cnt[0], jnp.full((tm - 2 * POOL_HALO, POOL_GROUP_WIDTH), 1.0 / win, F32), inv_cnt[1]],
                axis=0)
            pool_ref[0, :, sl] = (window * inv_cnt - u_main[:, sl]).astype(BF16)

    u_sub = []
    for si, (rs, h) in enumerate(zip(subs, hb)):
        u_sub.append(_dot(h, w_pool))
        ts = pl.ds(pl.multiple_of(i * tm + si * sub, sub), sub)
        cq, sq = cq_ref[ts, :], sq_ref[ts, :]
        for j in range(ATTN_WIDTH // MXU_WIDTH):
            pq = _dot(h, w_ref[:, j * MXU_WIDTH:(j + 1) * MXU_WIDTH])
            for e in range(heads_per_dot):
                t = pq[:, e * HEAD_DIM:(e + 1) * HEAD_DIM]
                q_ref[0, j * heads_per_dot + e, rs, :] = norm_rope(t, cq, sq)
        pk = _dot(h, w_ref[:, K_OFF:K_OFF + KV_WIDTH])
        pv = _dot(h, w_ref[:, V_OFF:V_OFF + KV_WIDTH])
        for e in range(N_KV_HEADS):
            sl = slice(e * HEAD_DIM, (e + 1) * HEAD_DIM)
            k_ref[0, e, rs, :] = norm_rope(pk[:, sl], ck_ref[ts, :], sk_ref[ts, :])
            v_ref[0, e, rs, :] = pv[:, sl].astype(BF16)
    pool_windows(u_sub)
    for rs, h in zip(subs, hb):
        for j in range(GATE_WIDTH // gate_chunk):
            sl = slice(j * gate_chunk, (j + 1) * gate_chunk)
            logits = _dot(h, w_ref[:, GATE_OFF + j * gate_chunk:GATE_OFF + (j + 1) * gate_chunk])
            gate_ref[0, rs, sl] = jax.nn.sigmoid(logits + bg_ref[:, sl]).astype(BF16)


def _in_proj(x, mod, g_pre, w_in, b_gate, rope_tabs, *, tm, sub):
    b, s, _ = x.shape
    n_tiles = s // tm
    halo_blocks_per_tile = tm // POOL_HALO
    last_halo_block = s // POOL_HALO - 1
    const2 = lambda bi, i: (0, 0)
    in_specs = [
        pl.BlockSpec((1, tm, D_MODEL), lambda bi, i: (bi, i, 0)),
        pl.BlockSpec((1, POOL_HALO, D_MODEL),
                     lambda bi, i: (bi, jnp.maximum(i * halo_blocks_per_tile - 1, 0), 0)),
        pl.BlockSpec((1, POOL_HALO, D_MODEL),
                     lambda bi, i: (bi, jnp.minimum((i + 1) * halo_blocks_per_tile, last_halo_block), 0)),
        pl.BlockSpec((1, N_MOD, D_MODEL), lambda bi, i: (bi, 0, 0)),
        pl.BlockSpec((1, D_MODEL), const2),
        _VMEM_RESIDENT,
        pl.BlockSpec((1, GATE_WIDTH), const2),
    ] + [_VMEM_RESIDENT] * len(rope_tabs)
    out_specs = [
        pl.BlockSpec((1, N_Q_HEADS, tm, HEAD_DIM), lambda bi, i: (bi, 0, i, 0)),
        pl.BlockSpec((1, N_KV_HEADS, tm, HEAD_DIM), lambda bi, i: (bi, 0, i, 0)),
        pl.BlockSpec((1, N_KV_HEADS, tm, HEAD_DIM), lambda bi, i: (bi, 0, i, 0)),
        pl.BlockSpec((1, tm, POOL_WIDTH), lambda bi, i: (bi, i, 0)),
        pl.BlockSpec((1, tm, GATE_WIDTH), lambda bi, i: (bi, i, 0)),
    ]
    out_shape = [
        jax.ShapeDtypeStruct((b, N_Q_HEADS, s, HEAD_DIM), BF16),
        jax.ShapeDtypeStruct((b, N_KV_HEADS, s, HEAD_DIM), BF16),
        jax.ShapeDtypeStruct((b, N_KV_HEADS, s, HEAD_DIM), BF16),
        jax.ShapeDtypeStruct((b, s, POOL_WIDTH), BF16),
        jax.ShapeDtypeStruct((b, s, GATE_WIDTH), BF16),
    ]
    return pl.pallas_call(
        functools.partial(_in_proj_kernel, tm=tm, sub=sub, n_tiles=n_tiles, seq=s),
        grid=(b, n_tiles),
        in_specs=in_specs,
        out_specs=out_specs,
        out_shape=out_shape,
        compiler_params=_params("parallel", "parallel"),
        name="in_proj",
    )(x, x, x, mod, g_pre, w_in, b_gate, *rope_tabs)


def _attention_kernel(q_ref, kc_ref, vc_ref, kl_ref, vl_ref, o_ref, *, tq, rows):
    kc = kc_ref[0, 0]
    kl = kl_ref[0, 0]
    ones_cols = lambda n: jnp.ones((n, MXU_WIDTH - HEAD_DIM), BF16)
    vc = jnp.concatenate([vc_ref[0, 0], ones_cols(kc.shape[0])], axis=-1)
    vl = jnp.concatenate([vl_ref[0, 0], ones_cols(kl.shape[0])], axis=-1)
    for h in range(Q_GROUP):
        for r in range(tq // rows):
            rs = slice(r * rows, (r + 1) * rows)
            q = q_ref[0, 0, h, rs, :]
            s_c = _dot_nt(q, kc)
            s_l = _dot_nt(q, kl)
            m = jnp.maximum(jnp.max(s_c, axis=-1, keepdims=True),
                            jnp.max(s_l, axis=-1, keepdims=True))
            p_c = jnp.exp2(s_c - m).astype(BF16)
            p_l = jnp.exp2(s_l - m).astype(BF16)
            o = _dot(p_c, vc) + _dot(p_l, vl)
            o_ref[0, rs, h * HEAD_DIM:(h + 1) * HEAD_DIM] = (
                o[:, :HEAD_DIM] / o[:, HEAD_DIM:]).astype(BF16)


def _attention(q, k_ctx, v_ctx, k_lat, v_lat, *, tq, rows):
    b, _, s, _ = q.shape
    n_ctx = k_ctx.shape[2]
    qg = q.reshape(b, N_KV_HEADS, Q_GROUP, s, HEAD_DIM)
    kv_idx = lambda bi, kh, i: (bi, kh, 0, 0)
    return pl.pallas_call(
        functools.partial(_attention_kernel, tq=tq, rows=rows),
        grid=(b, N_KV_HEADS, s // tq),
        in_specs=[pl.BlockSpec((1, 1, Q_GROUP, tq, HEAD_DIM), lambda bi, kh, i: (bi, kh, 0, i, 0)),
                  pl.BlockSpec((1, 1, n_ctx, HEAD_DIM), kv_idx),
                  pl.BlockSpec((1, 1, n_ctx, HEAD_DIM), kv_idx),
                  pl.BlockSpec((1, 1, s, HEAD_DIM), kv_idx),
                  pl.BlockSpec((1, 1, s, HEAD_DIM), kv_idx)],
        out_specs=pl.BlockSpec((1, tq, Q_GROUP * HEAD_DIM), lambda bi, kh, i: (bi, i, kh)),
        out_shape=jax.ShapeDtypeStruct((b, s, ATTN_WIDTH), BF16),
        compiler_params=_params("parallel", "parallel", "parallel"),
        name="attention",
    )(qg, k_ctx, v_ctx, k_lat, v_lat)


def _merge_mlp_kernel(x_ref, a_ref, p_ref, g_ref, mod_ref, gmix_ref, gpre_ref, gpost_ref,
                      wa_ref, wp_ref, wo_ref, w1_ref, w2_ref, o_ref, *, tm, sub, ff_chunk):
    subs = [slice(s * sub, (s + 1) * sub) for s in range(tm // sub)]
    x_mid = []
    for rs in subs:
        ya = _dot(a_ref[0, rs, :], wa_ref[...])
        yp = _dot(p_ref[0, rs, :], wp_ref[...])
        ga = g_ref[0, rs, :D_MODEL].astype(F32)
        gp = g_ref[0, rs, D_MODEL:].astype(F32)
        y = _dot((ga * ya + gp * yp).astype(BF16), wo_ref[...])
        x_mid.append(x_ref[0, rs, :] + mod_ref[0, 2:3, :] * _rms(y, gmix_ref[...]))
    for rs, xs in zip(subs, x_mid):
        h = _norm_mod(xs, gpre_ref[...], mod_ref[0, 3:4, :], mod_ref[0, 4:5, :]).astype(BF16)
        acc = None
        for j in range(D_FF // ff_chunk):
            sl = slice(j * ff_chunk, (j + 1) * ff_chunk)
            t = jnp.maximum(_dot(h, w1_ref[:, sl]), 0.0)
            part = _dot((t * t).astype(BF16), w2_ref[sl, :])
            acc = part if acc is None else acc + part
        o_ref[0, rs, :] = xs + mod_ref[0, 5:6, :] * _rms(acc, gpost_ref[...])


def _merge_mlp(x, attn_o, pooled, gates, mod, g_post_mix, g_pre_mlp, g_post_mlp,
               w_attn_up, w_pool_fold, w_out, w_ff1, w_ff2, *, tm, sub, ff_chunk):
    b, s, _ = x.shape
    tile = lambda width: pl.BlockSpec((1, tm, width), lambda bi, i: (bi, i, 0))
    const2 = lambda bi, i: (0, 0)
    gain = pl.BlockSpec((1, D_MODEL), const2)
    return pl.pallas_call(
        functools.partial(_merge_mlp_kernel, tm=tm, sub=sub, ff_chunk=ff_chunk),
        grid=(b, s // tm),
        in_specs=[tile(D_MODEL), tile(ATTN_WIDTH), tile(POOL_WIDTH), tile(GATE_WIDTH),
                  pl.BlockSpec((1, N_MOD, D_MODEL), lambda bi, i: (bi, 0, 0)),
                  gain, gain, gain] + [_VMEM_RESIDENT] * 5,
        out_specs=tile(D_MODEL),
        out_shape=jax.ShapeDtypeStruct(x.shape, F32),
        compiler_params=_params("parallel", "parallel"),
        name="merge_mlp",
    )(x, attn_o, pooled, gates, mod, g_post_mix, g_pre_mlp, g_post_mlp,
      w_attn_up, w_pool_fold, w_out, w_ff1, w_ff2)


def _rope_tables(seq):
    t = np.arange(seq)
    rows = (t // GRID_W).astype(np.float64)
    cols = (t % GRID_W).astype(np.float64)
    freqs = ROPE_THETA ** (-np.arange(ROPE_FREQS, dtype=np.float64) / ROPE_FREQS)
    ang_row = rows[:, None] * freqs
    ang_col = cols[:, None] * freqs
    cos_r, sin_r, cos_c, sin_c = np.cos(ang_row), np.sin(ang_row), np.cos(ang_col), np.sin(ang_col)
    cos_t = np.concatenate([cos_r, cos_c, cos_r, cos_c], axis=-1).astype(np.float32)
    sin_t = np.concatenate([-sin_r, -sin_c, sin_r, sin_c], axis=-1).astype(np.float32)
    return jnp.asarray(cos_t), jnp.asarray(sin_t)


def kernel(x, c, ctx, c_ctx, w_mod, b_mod, g_pre_mix, g_post_mix, g_pre_mlp, g_post_mlp, w_in, b_gate, g_q, g_k, w_attn_up, w_pool_grp, pool_scale, w_pool_up, w_out, w_ff1, w_ff2):
    depth = w_mod.shape[0]
    assert depth == 1, "single-layer block"
    b, s, _ = x.shape
    row = lambda a: a[0].reshape(1, -1)

    n_cond = -(-(b + 1) // F32_SUBLANES) * F32_SUBLANES
    cond = jnp.concatenate([c, c_ctx[None, :], jnp.zeros((n_cond - b - 1, D_MODEL), F32)], axis=0)
    mod_all = _adaln(cond, w_mod[0], b_mod[0])
    mod = mod_all[:b].reshape(b, N_MOD, D_MODEL)
    mod_ctx = mod_all[b:b + 1].reshape(1, N_MOD, D_MODEL)

    w_in_b = w_in[0].astype(BF16)
    w_in_b = jnp.concatenate([_permute_head_dim(w_in_b[:, :V_OFF]), w_in_b[:, V_OFF:]], axis=-1)
    g_q_p = _permute_head_dim(row(g_q))
    g_k_p = _permute_head_dim(row(g_k))
    cos_t, sin_t = _rope_tables(s)

    k_ctx, v_ctx = _ctx_kv(ctx, mod_ctx, row(g_pre_mix), w_in_b[:, K_OFF:POOL_OFF], g_k_p, samples=4)
    rope_tabs = _rope_fold(cos_t, sin_t, g_q_p, g_k_p)
    q, k_lat, v_lat, pooled, gates = _in_proj(
        x, mod, row(g_pre_mix), w_in_b, row(b_gate), rope_tabs, tm=1024, sub=256)
    attn_o = _attention(q, k_ctx, v_ctx, k_lat, v_lat, tq=1024, rows=128)
    w_pool_fold = _pool_fold(w_pool_grp[0], row(pool_scale), w_pool_up[0])
    return _merge_mlp(x, attn_o, pooled, gates, mod, row(g_post_mix), row(g_pre_mlp), row(g_post_mlp),
                      w_attn_up[0].astype(BF16), w_pool_fold, w_out[0].astype(BF16),
                      w_ff1[0].astype(BF16), w_ff2[0].astype(BF16), tm=1024, sub=256, ff_chunk=1024)
```

```python
import functools
import math

import jax
import jax.numpy as jnp
import numpy as np
from jax import lax
from jax.experimental import pallas as pl
from jax.experimental.pallas import tpu as pltpu

D_MODEL = 1024
GRID_W = 64
HEAD_DIM = 128
N_Q_HEADS = 8
N_KV_HEADS = 2
Q_GROUP = N_Q_HEADS // N_KV_HEADS
ATTN_WIDTH = N_Q_HEADS * HEAD_DIM
KV_WIDTH = N_KV_HEADS * HEAD_DIM
POOL_WINDOWS = (2, 4, 8, 16)
POOL_WIDTH = 512
POOL_GROUP_WIDTH = POOL_WIDTH // len(POOL_WINDOWS)
GATE_WIDTH = 2 * D_MODEL
K_OFF = ATTN_WIDTH
V_OFF = K_OFF + KV_WIDTH
POOL_OFF = V_OFF + KV_WIDTH
GATE_OFF = POOL_OFF + POOL_WIDTH
D_FF = 4 * D_MODEL
ROPE_THETA = 10000.0
ROPE_FREQS = HEAD_DIM // 4
ROPE_PARTNER = HEAD_DIM // 2
EPS = 1e-6
N_MOD = 6

F32_SUBLANES = 8
MXU_WIDTH = 256
VMEM_LIMIT_BYTES = 62 * 1024 * 1024

POOL_HALO = max(POOL_WINDOWS) // 2
assert POOL_HALO == F32_SUBLANES

Q_SCALE = (HEAD_DIM ** -0.5) * math.log2(math.e)

BF16 = jnp.bfloat16
F32 = jnp.float32


def _dot(a, b):
    return jnp.dot(a, b, preferred_element_type=F32)


def _dot_nt(a, b):
    return lax.dot_general(a, b, (((1,), (1,)), ((), ())), preferred_element_type=F32)


def _rms(x, gain):
    ms = jnp.mean(x * x, axis=-1, keepdims=True)
    return x * lax.rsqrt(ms + EPS) * gain


def _norm_mod(x, gain, shift, scale):
    return _rms(x, gain * (1.0 + scale)) + shift


def _permute_head_dim(a):
    lead = a.shape[:-1]
    blocks = a.reshape(*lead, -1, 2, 2, ROPE_FREQS)
    return jnp.swapaxes(blocks, -3, -2).reshape(a.shape)


_VMEM_RESIDENT = pl.BlockSpec(memory_space=pltpu.VMEM)


def _params(*sem):
    return pltpu.CompilerParams(dimension_semantics=sem, vmem_limit_bytes=VMEM_LIMIT_BYTES)


def _adaln_kernel(c_ref, w_ref, b_ref, o_ref):
    c = c_ref[...]
    a = (c * jax.nn.sigmoid(c)).astype(BF16)
    o_ref[...] = _dot(a, w_ref[...].astype(BF16)) + b_ref[...]


def _adaln(cond, w_mod, b_mod):
    rows = cond.shape[0]
    n_out = w_mod.shape[1]
    bn = D_MODEL
    return pl.pallas_call(
        _adaln_kernel,
        grid=(n_out // bn,),
        in_specs=[pl.BlockSpec((rows, D_MODEL), lambda j: (0, 0)),
                  pl.BlockSpec((D_MODEL, bn), lambda j: (0, j)),
                  pl.BlockSpec((1, bn), lambda j: (0, j))],
        out_specs=pl.BlockSpec((rows, bn), lambda j: (0, j)),
        out_shape=jax.ShapeDtypeStruct((rows, n_out), F32),
        compiler_params=_params("parallel"),
        name="adaln",
    )(cond, w_mod, b_mod.reshape(1, n_out))


def _ctx_kv_kernel(x_ref, mod_ref, gpre_ref, w_ref, gk_ref, k_ref, v_ref):
    for s in range(x_ref.shape[0]):
        h = _norm_mod(x_ref[s], gpre_ref[...], mod_ref[0, 0:1, :], mod_ref[0, 1:2, :]).astype(BF16)
        pk = _dot(h, w_ref[:, :KV_WIDTH])
        pv = _dot(h, w_ref[:, KV_WIDTH:])
        for e in range(N_KV_HEADS):
            sl = slice(e * HEAD_DIM, (e + 1) * HEAD_DIM)
            k_ref[s, e] = _rms(pk[:, sl], gk_ref[...]).astype(BF16)
            v_ref[s, e] = pv[:, sl].astype(BF16)


def _ctx_kv(ctx, mod_ctx, g_pre, w_kv, g_k, *, samples):
    b, n, _ = ctx.shape
    kv_shape = jax.ShapeDtypeStruct((b, N_KV_HEADS, n, HEAD_DIM), BF16)
    kv_spec = pl.BlockSpec((samples, N_KV_HEADS, n, HEAD_DIM), lambda i: (i, 0, 0, 0))
    return pl.pallas_call(
        _ctx_kv_kernel,
        grid=(b // samples,),
        in_specs=[pl.BlockSpec((samples, n, D_MODEL), lambda i: (i, 0, 0)),
                  pl.BlockSpec((1, N_MOD, D_MODEL), lambda i: (0, 0, 0)),
                  pl.BlockSpec((1, D_MODEL), lambda i: (0, 0)),
                  pl.BlockSpec((D_MODEL, 2 * KV_WIDTH), lambda i: (0, 0)),
                  pl.BlockSpec((1, HEAD_DIM), lambda i: (0, 0))],
        out_specs=[kv_spec, kv_spec],
        out_shape=[kv_shape, kv_shape],
        compiler_params=_params("parallel"),
        name="ctx_kv",
    )(ctx, mod_ctx, g_pre, w_kv, g_k)


def _pool_fold_kernel(wg_ref, ps_ref, wu_ref, o_ref):
    for g in range(len(POOL_WINDOWS)):
        sl = slice(g * POOL_GROUP_WIDTH, (g + 1) * POOL_GROUP_WIDTH)
        o_ref[sl, :] = jnp.dot(wg_ref[g] * ps_ref[:, sl], wu_ref[sl, :], precision=lax.Precision.HIGHEST,
                               preferred_element_type=F32).astype(BF16)


def _pool_fold(w_grp, pool_scale, w_pool_up):
    return pl.pallas_call(
        _pool_fold_kernel,
        out_shape=jax.ShapeDtypeStruct(w_pool_up.shape, BF16),
        compiler_params=pltpu.CompilerParams(vmem_limit_bytes=VMEM_LIMIT_BYTES),
        name="pool_fold",
    )(w_grp, pool_scale, w_pool_up)


def _rope_fold_kernel(cos_ref, sin_ref, gq_ref, gk_ref, cq_ref, sq_ref, ck_ref, sk_ref):
    cos = cos_ref[...]
    sin = sin_ref[...]
    for g_ref, c_ref, s_ref, logit_scale in ((gq_ref, cq_ref, sq_ref, Q_SCALE), (gk_ref, ck_ref, sk_ref, 1.0)):
        g = jnp.broadcast_to(g_ref[...], (F32_SUBLANES, HEAD_DIM)) * (logit_scale * math.sqrt(HEAD_DIM))
        g_partner = pltpu.roll(g, ROPE_PARTNER, 1)
        c_ref[...] = cos * g[0:1, :]
        s_ref[...] = sin * g_partner[0:1, :]


def _rope_fold(cos_t, sin_t, g_q, g_k):
    tab = jax.ShapeDtypeStruct(cos_t.shape, F32)
    return pl.pallas_call(
        _rope_fold_kernel,
        out_shape=[tab] * 4,
        compiler_params=pltpu.CompilerParams(vmem_limit_bytes=VMEM_LIMIT_BYTES),
        name="rope_fold",
    )(cos_t, sin_t, g_q, g_k)


def _in_proj_kernel(x_ref, xp_ref, xn_ref, mod_ref, gpre_ref, w_ref, bg_ref,
                    cq_ref, sq_ref, ck_ref, sk_ref, q_ref, k_ref, v_ref, pool_ref, gate_ref,
                    *, tm, sub, n_tiles, seq):
    i = pl.program_id(1)
    gpre = gpre_ref[...]
    shift = mod_ref[0, 0:1, :]
    scale = mod_ref[0, 1:2, :]
    subs = [slice(s * sub, (s + 1) * sub) for s in range(tm // sub)]
    hb = [_norm_mod(x_ref[0, rs, :], gpre, shift, scale).astype(BF16) for rs in subs]

    w_pool = w_ref[:, POOL_OFF:POOL_OFF + POOL_WIDTH]
    heads_per_dot = MXU_WIDTH // HEAD_DIM
    gate_chunk = GATE_WIDTH // (ATTN_WIDTH // MXU_WIDTH)

    def norm_rope(t, cos_g, sin_g):
        inv = lax.rsqrt(jnp.sum(t * t, axis=-1, keepdims=True) + HEAD_DIM * EPS)
        return ((t * cos_g + pltpu.roll(t, ROPE_PARTNER, 1) * sin_g) * inv).astype(BF16)

    def pool_windows(u_sub):
        u_main = jnp.concatenate(u_sub, axis=0)
        x_halo = jnp.concatenate([xp_ref[0], xn_ref[0]], axis=0)
        u_halo = _dot(_norm_mod(x_halo, gpre, shift, scale).astype(BF16), w_pool)
        u_prev = jnp.where(i > 0, u_halo[:POOL_HALO], 0.0)
        u_next = jnp.where(i < n_tiles - 1, u_halo[POOL_HALO:], 0.0)
        u_ext = jnp.concatenate([u_prev, u_main, u_next], axis=0)
        n_ext = tm + 2 * POOL_HALO
        edge_iota = lax.broadcasted_iota(jnp.int32, (POOL_HALO, POOL_GROUP_WIDTH), 0)
        edge_t = (i * tm + edge_iota, i * tm + (tm - POOL_HALO) + edge_iota)
        for gi, win in enumerate(POOL_WINDOWS):
            sl = slice(gi * POOL_GROUP_WIDTH, (gi + 1) * POOL_GROUP_WIDTH)
            half = win // 2
            run = u_ext[:, sl]
            length = 1
            while length < half:
                run = run + pltpu.roll(run, n_ext - length, 0)
                length *= 2
            window = run + pltpu.roll(run, half, 0)
            window = window[POOL_HALO:POOL_HALO + tm]
            inv_cnt = [1.0 / (jnp.minimum(t + half, seq) - jnp.maximum(t - half, 0)).astype(F32)
                       for t in edge_t]
            inv_cnt = jnp.concatenate(
                [inv_cnt[0], jnp.full((tm - 2 * POOL_HALO, POOL_GROUP_WIDTH), 1.0 / win, F32), inv_cnt[1]],
                axis=0)
            pool_ref[0, :, sl] = (window * inv_cnt - u_main[:, sl]).astype(BF16)

    u_sub = []
    for si, (rs, h) in enumerate(zip(subs, hb)):
        u_sub.append(_dot(h, w_pool))
        ts = pl.ds(pl.multiple_of(i * tm + si * sub, sub), sub)
        cq, sq = cq_ref[ts, :], sq_ref[ts, :]
        for j in range(ATTN_WIDTH // MXU_WIDTH):
            pq = _dot(h, w_ref[:, j * MXU_WIDTH:(j + 1) * MXU_WIDTH])
            for e in range(heads_per_dot):
                t = pq[:, e * HEAD_DIM:(e + 1) * HEAD_DIM]
                q_ref[0, j * heads_per_dot + e, rs, :] = norm_rope(t, cq, sq)
        pk = _dot(h, w_ref[:, K_OFF:K_OFF + KV_WIDTH])
        pv = _dot(h, w_ref[:, V_OFF:V_OFF + KV_WIDTH])
        for e in range(N_KV_HEADS):
            sl = slice(e * HEAD_DIM, (e + 1) * HEAD_DIM)
            k_ref[0, e, rs, :] = norm_rope(pk[:, sl], ck_ref[ts, :], sk_ref[ts, :])
            v_ref[0, e, rs, :] = pv[:, sl].astype(BF16)
    pool_windows(u_sub)
    for rs, h in zip(subs, hb):
        for j in range(GATE_WIDTH // gate_chunk):
            sl = slice(j * gate_chunk, (j + 1) * gate_chunk)
            logits = _dot(h, w_ref[:, GATE_OFF + j * gate_chunk:GATE_OFF + (j + 1) * gate_chunk])
            gate_ref[0, rs, sl] = jax.nn.sigmoid(logits + bg_ref[:, sl]).astype(BF16)


def _in_proj(x, mod, g_pre, w_in, b_gate, rope_tabs, *, tm, sub):
    b, s, _ = x.shape
    n_tiles = s // tm
    halo_blocks_per_tile = tm // POOL_HALO
    last_halo_block = s // POOL_HALO - 1
    const2 = lambda bi, i: (0, 0)
    in_specs = [
        pl.BlockSpec((1, tm, D_MODEL), lambda bi, i: (bi, i, 0)),
        pl.BlockSpec((1, POOL_HALO, D_MODEL),
                     lambda bi, i: (bi, jnp.maximum(i * halo_blocks_per_tile - 1, 0), 0)),
        pl.BlockSpec((1, POOL_HALO, D_MODEL),
                     lambda bi, i: (bi, jnp.minimum((i + 1) * halo_blocks_per_tile, last_halo_block), 0)),
        pl.BlockSpec((1, N_MOD, D_MODEL), lambda bi, i: (bi, 0, 0)),
        pl.BlockSpec((1, D_MODEL), const2),
        _VMEM_RESIDENT,
        pl.BlockSpec((1, GATE_WIDTH), const2),
    ] + [_VMEM_RESIDENT] * len(rope_tabs)
    out_specs = [
        pl.BlockSpec((1, N_Q_HEADS, tm, HEAD_DIM), lambda bi, i: (bi, 0, i, 0)),
        pl.BlockSpec((1, N_KV_HEADS, tm, HEAD_DIM), lambda bi, i: (bi, 0, i, 0)),
        pl.BlockSpec((1, N_KV_HEADS, tm, HEAD_DIM), lambda bi, i: (bi, 0, i, 0)),
        pl.BlockSpec((1, tm, POOL_WIDTH), lambda bi, i: (bi, i, 0)),
        pl.BlockSpec((1, tm, GATE_WIDTH), lambda bi, i: (bi, i, 0)),
    ]
    out_shape = [
        jax.ShapeDtypeStruct((b, N_Q_HEADS, s, HEAD_DIM), BF16),
        jax.ShapeDtypeStruct((b, N_KV_HEADS, s, HEAD_DIM), BF16),
        jax.ShapeDtypeStruct((b, N_KV_HEADS, s, HEAD_DIM), BF16),
        jax.ShapeDtypeStruct((b, s, POOL_WIDTH), BF16),
        jax.ShapeDtypeStruct((b, s, GATE_WIDTH), BF16),
    ]
    return pl.pallas_call(
        functools.partial(_in_proj_kernel, tm=tm, sub=sub, n_tiles=n_tiles, seq=s),
        grid=(b, n_tiles),
        in_specs=in_specs,
        out_specs=out_specs,
        out_shape=out_shape,
        compiler_params=_params("parallel", "parallel"),
        name="in_proj",
    )(x, x, x, mod, g_pre, w_in, b_gate, *rope_tabs)


def _attention_kernel(q_ref, kc_ref, vc_ref, kl_ref, vl_ref, o_ref, *, tq, rows):
    kc = kc_ref[0, 0]
    kl = kl_ref[0, 0]
    ones_cols = lambda n: jnp.ones((n, MXU_WIDTH - HEAD_DIM), BF16)
    vc = jnp.concatenate([vc_ref[0, 0], ones_cols(kc.shape[0])], axis=-1)
    vl = jnp.concatenate([vl_ref[0, 0], ones_cols(kl.shape[0])], axis=-1)
    for h in range(Q_GROUP):
        for r in range(tq // rows):
            rs = slice(r * rows, (r + 1) * rows)
            q = q_ref[0, 0, h, rs, :]
            s_c = _dot_nt(q, kc)
            s_l = _dot_nt(q, kl)
            m = jnp.maximum(jnp.max(s_c, axis=-1, keepdims=True),
                            jnp.max(s_l, axis=-1, keepdims=True))
            p_c = jnp.exp2(s_c - m).astype(BF16)
            p_l = jnp.exp2(s_l - m).astype(BF16)
            o = _dot(p_c, vc) + _dot(p_l, vl)
            o_ref[0, rs, h * HEAD_DIM:(h + 1) * HEAD_DIM] = (
                o[:, :HEAD_DIM] / o[:, HEAD_DIM:]).astype(BF16)


def _attention(q, k_ctx, v_ctx, k_lat, v_lat, *, tq, rows):
    b, _, s, _ = q.shape
    n_ctx = k_ctx.shape[2]
    qg = q.reshape(b, N_KV_HEADS, Q_GROUP, s, HEAD_DIM)
    kv_idx = lambda bi, kh, i: (bi, kh, 0, 0)
    return pl.pallas_call(
        functools.partial(_attention_kernel, tq=tq, rows=rows),
        grid=(b, N_KV_HEADS, s // tq),
        in_specs=[pl.BlockSpec((1, 1, Q_GROUP, tq, HEAD_DIM), lambda bi, kh, i: (bi, kh, 0, i, 0)),
                  pl.BlockSpec((1, 1, n_ctx, HEAD_DIM), kv_idx),
                  pl.BlockSpec((1, 1, n_ctx, HEAD_DIM), kv_idx),
                  pl.BlockSpec((1, 1, s, HEAD_DIM), kv_idx),
                  pl.BlockSpec((1, 1, s, HEAD_DIM), kv_idx)],
        out_specs=pl.BlockSpec((1, tq, Q_GROUP * HEAD_DIM), lambda bi, kh, i: (bi, i, kh)),
        out_shape=jax.ShapeDtypeStruct((b, s, ATTN_WIDTH), BF16),
        compiler_params=_params("parallel", "parallel", "parallel"),
        name="attention",
    )(qg, k_ctx, v_ctx, k_lat, v_lat)


def _merge_mlp_kernel(x_ref, a_ref, p_ref, g_ref, mod_ref, gmix_ref, gpre_ref, gpost_ref,
                      wa_ref, wp_ref, wo_ref, w1_ref, w2_ref, o_ref, *, tm, sub, ff_chunk):
    subs = [slice(s * sub, (s + 1) * sub) for s in range(tm // sub)]
    x_mid = []
    for rs in subs:
        ya = _dot(a_ref[0, rs, :], wa_ref[...])
        yp = _dot(p_ref[0, rs, :], wp_ref[...])
        ga = g_ref[0, rs, :D_MODEL].astype(F32)
        gp = g_ref[0, rs, D_MODEL:].astype(F32)
        y = _dot((ga * ya + gp * yp).astype(BF16), wo_ref[...])
        x_mid.append(x_ref[0, rs, :] + mod_ref[0, 2:3, :] * _rms(y, gmix_ref[...]))
    for rs, xs in zip(subs, x_mid):
        h = _norm_mod(xs, gpre_ref[...], mod_ref[0, 3:4, :], mod_ref[0, 4:5, :]).astype(BF16)
        acc = None
        for j in range(D_FF // ff_chunk):
            sl = slice(j * ff_chunk, (j + 1) * ff_chunk)
            t = jnp.maximum(_dot(h, w1_ref[:, sl]), 0.0)
            part = _dot((t * t).astype(BF16), w2_ref[sl, :])
            acc = part if acc is None else acc + part
        o_ref[0, rs, :] = xs + mod_ref[0, 5:6, :] * _rms(acc, gpost_ref[...])


def _merge_mlp(x, attn_o, pooled, gates, mod, g_post_mix, g_pre_mlp, g_post_mlp,
               w_attn_up, w_pool_fold, w_out, w_ff1, w_ff2, *, tm, sub, ff_chunk):
    b, s, _ = x.shape
    tile = lambda width: pl.BlockSpec((1, tm, width), lambda bi, i: (bi, i, 0))
    const2 = lambda bi, i: (0, 0)
    gain = pl.BlockSpec((1, D_MODEL), const2)
    return pl.pallas_call(
        functools.partial(_merge_mlp_kernel, tm=tm, sub=sub, ff_chunk=ff_chunk),
        grid=(b, s // tm),
        in_specs=[tile(D_MODEL), tile(ATTN_WIDTH), tile(POOL_WIDTH), tile(GATE_WIDTH),
                  pl.BlockSpec((1, N_MOD, D_MODEL), lambda bi, i: (bi, 0, 0)),
                  gain, gain, gain] + [_VMEM_RESIDENT] * 5,
        out_specs=tile(D_MODEL),
        out_shape=jax.ShapeDtypeStruct(x.shape, F32),
        compiler_params=_params("parallel", "parallel"),
        name="merge_mlp",
    )(x, attn_o, pooled, gates, mod, g_post_mix, g_pre_mlp, g_post_mlp,
      w_attn_up, w_pool_fold, w_out, w_ff1, w_ff2)


def _rope_tables(seq):
    t = np.arange(seq)
    rows = (t // GRID_W).astype(np.float64)
    cols = (t % GRID_W).astype(np.float64)
    freqs = ROPE_THETA ** (-np.arange(ROPE_FREQS, dtype=np.float64) / ROPE_FREQS)
    ang_row = rows[:, None] * freqs
    ang_col = cols[:, None] * freqs
    cos_r, sin_r, cos_c, sin_c = np.cos(ang_row), np.sin(ang_row), np.cos(ang_col), np.sin(ang_col)
    cos_t = np.concatenate([cos_r, cos_c, cos_r, cos_c], axis=-1).astype(np.float32)
    sin_t = np.concatenate([-sin_r, -sin_c, sin_r, sin_c], axis=-1).astype(np.float32)
    return jnp.asarray(cos_t), jnp.asarray(sin_t)


def kernel(x, c, ctx, c_ctx, w_mod, b_mod, g_pre_mix, g_post_mix, g_pre_mlp, g_post_mlp, w_in, b_gate, g_q, g_k, w_attn_up, w_pool_grp, pool_scale, w_pool_up, w_out, w_ff1, w_ff2):
    depth = w_mod.shape[0]
    assert depth == 1, "single-layer block"
    b, s, _ = x.shape
    row = lambda a: a[0].reshape(1, -1)

    n_cond = -(-(b + 1) // F32_SUBLANES) * F32_SUBLANES
    cond = jnp.concatenate([c, c_ctx[None, :], jnp.zeros((n_cond - b - 1, D_MODEL), F32)], axis=0)
    mod_all = _adaln(cond, w_mod[0], b_mod[0])
    mod = mod_all[:b].reshape(b, N_MOD, D_MODEL)
    mod_ctx = mod_all[b:b + 1].reshape(1, N_MOD, D_MODEL)

    w_in_b = w_in[0].astype(BF16)
    w_in_b = jnp.concatenate([_permute_head_dim(w_in_b[:, :V_OFF]), w_in_b[:, V_OFF:]], axis=-1)
    g_q_p = _permute_head_dim(row(g_q))
    g_k_p = _permute_head_dim(row(g_k))
    cos_t, sin_t = _rope_tables(s)

    k_ctx, v_ctx = _ctx_kv(ctx, mod_ctx, row(g_pre_mix), w_in_b[:, K_OFF:POOL_OFF], g_k_p, samples=8)
    rope_tabs = _rope_fold(cos_t, sin_t, g_q_p, g_k_p)
    q, k_lat, v_lat, pooled, gates = _in_proj(
        x, mod, row(g_pre_mix), w_in_b, row(b_gate), rope_tabs, tm=1024, sub=256)
    attn_o = _attention(q, k_ctx, v_ctx, k_lat, v_lat, tq=2048, rows=128)
    w_pool_fold = _pool_fold(w_pool_grp[0], row(pool_scale), w_pool_up[0])
    return _merge_mlp(x, attn_o, pooled, gates, mod, row(g_post_mix), row(g_pre_mlp), row(g_post_mlp),
                      w_attn_up[0].astype(BF16), w_pool_fold, w_out[0].astype(BF16),
                      w_ff1[0].astype(BF16), w_ff2[0].astype(BF16), tm=1024, sub=256, ff_chunk=1024)
```

```python
import functools
import math

import jax
import jax.numpy as jnp
import numpy as np
from jax import lax
from jax.experimental import pallas as pl
from jax.experimental.pallas import tpu as pltpu

D_MODEL = 1024
GRID_W = 64
HEAD_DIM = 128
N_Q_HEADS = 8
N_KV_HEADS = 2
Q_GROUP = N_Q_HEADS // N_KV_HEADS
ATTN_WIDTH = N_Q_HEADS * HEAD_DIM
KV_WIDTH = N_KV_HEADS * HEAD_DIM
POOL_WINDOWS = (2, 4, 8, 16)
POOL_WIDTH = 512
POOL_GROUP_WIDTH = POOL_WIDTH // len(POOL_WINDOWS)
GATE_WIDTH = 2 * D_MODEL
K_OFF = ATTN_WIDTH
V_OFF = K_OFF + KV_WIDTH
POOL_OFF = V_OFF + KV_WIDTH
GATE_OFF = POOL_OFF + POOL_WIDTH
D_FF = 4 * D_MODEL
ROPE_THETA = 10000.0
ROPE_FREQS = HEAD_DIM // 4
ROPE_PARTNER = HEAD_DIM // 2
EPS = 1e-6
N_MOD = 6

F32_SUBLANES = 8
MXU_WIDTH = 256
VMEM_LIMIT_BYTES = 62 * 1024 * 1024

POOL_HALO = max(POOL_WINDOWS) // 2
assert POOL_HALO == F32_SUBLANES

Q_SCALE = (HEAD_DIM ** -0.5) * math.log2(math.e)

BF16 = jnp.bfloat16
F32 = jnp.float32


def _dot(a, b):
    return jnp.dot(a, b, preferred_element_type=F32)


def _dot_nt(a, b):
    return lax.dot_general(a, b, (((1,), (1,)), ((), ())), preferred_element_type=F32)


def _rms(x, gain):
    ms = jnp.mean(x * x, axis=-1, keepdims=True)
    return x * lax.rsqrt(ms + EPS) * gain


def _norm_mod(x, gain, shift, scale):
    return _rms(x, gain * (1.0 + scale)) + shift


def _permute_head_dim(a):
    lead = a.shape[:-1]
    blocks = a.reshape(*lead, -1, 2, 2, ROPE_FREQS)
    return jnp.swapaxes(blocks, -3, -2).reshape(a.shape)


_VMEM_RESIDENT = pl.BlockSpec(memory_space=pltpu.VMEM)


def _params(*sem):
    return pltpu.CompilerParams(dimension_semantics=sem, vmem_limit_bytes=VMEM_LIMIT_BYTES)


def _w_in_prep_kernel(w_ref, o_ref, *, bn):
    is_qk = pl.program_id(0) * bn < V_OFF
    block = lax.broadcasted_iota(jnp.int32, (w_ref.shape[0], HEAD_DIM), 1) // ROPE_FREQS
    for e in range(bn // HEAD_DIM):
        sl = slice(e * HEAD_DIM, (e + 1) * HEAD_DIM)
        t = w_ref[:, sl]
        swapped = jnp.where(block == 1, pltpu.roll(t, HEAD_DIM - ROPE_FREQS, 1),
                            jnp.where(block == 2, pltpu.roll(t, ROPE_FREQS, 1), t))
        o_ref[:, sl] = jnp.where(is_qk, swapped, t).astype(BF16)


def _w_in_prep(w_in):
    k, n = w_in.shape
    bn = MXU_WIDTH
    assert V_OFF % bn == 0
    return pl.pallas_call(
        functools.partial(_w_in_prep_kernel, bn=bn),
        grid=(n // bn,),
        in_specs=[pl.BlockSpec((k, bn), lambda j: (0, j))],
        out_specs=pl.BlockSpec((k, bn), lambda j: (0, j)),
        out_shape=jax.ShapeDtypeStruct((k, n), BF16),
        compiler_params=_params("parallel"),
        name="w_in_prep",
    )(w_in)


def _adaln_kernel(c_ref, w_ref, b_ref, o_ref):
    c = c_ref[...]
    a = (c * jax.nn.sigmoid(c)).astype(BF16)
    o_ref[...] = _dot(a, w_ref[...].astype(BF16)) + b_ref[...]


def _adaln(cond, w_mod, b_mod):
    rows = cond.shape[0]
    n_out = w_mod.shape[1]
    bn = D_MODEL
    return pl.pallas_call(
        _adaln_kernel,
        grid=(n_out // bn,),
        in_specs=[pl.BlockSpec((rows, D_MODEL), lambda j: (0, 0)),
                  pl.BlockSpec((D_MODEL, bn), lambda j: (0, j)),
                  pl.BlockSpec((1, bn), lambda j: (0, j))],
        out_specs=pl.BlockSpec((rows, bn), lambda j: (0, j)),
        out_shape=jax.ShapeDtypeStruct((rows, n_out), F32),
        compiler_params=_params("parallel"),
        name="adaln",
    )(cond, w_mod, b_mod.reshape(1, n_out))


def _ctx_kv_kernel(x_ref, mod_ref, gpre_ref, w_ref, gk_ref, k_ref, v_ref):
    for s in range(x_ref.shape[0]):
        h = _norm_mod(x_ref[s], gpre_ref[...], mod_ref[0, 0:1, :], mod_ref[0, 1:2, :]).astype(BF16)
        pk = _dot(h, w_ref[:, :KV_WIDTH])
        pv = _dot(h, w_ref[:, KV_WIDTH:])
        for e in range(N_KV_HEADS):
            sl = slice(e * HEAD_DIM, (e + 1) * HEAD_DIM)
            k_ref[s, e] = _rms(pk[:, sl], gk_ref[...]).astype(BF16)
            v_ref[s, e] = pv[:, sl].astype(BF16)


def _ctx_kv(ctx, mod_ctx, g_pre, w_kv, g_k, *, samples):
    b, n, _ = ctx.shape
    kv_shape = jax.ShapeDtypeStruct((b, N_KV_HEADS, n, HEAD_DIM), BF16)
    kv_spec = pl.BlockSpec((samples, N_KV_HEADS, n, HEAD_DIM), lambda i: (i, 0, 0, 0))
    return pl.pallas_call(
        _ctx_kv_kernel,
        grid=(b // samples,),
        in_specs=[pl.BlockSpec((samples, n, D_MODEL), lambda i: (i, 0, 0)),
                  pl.BlockSpec((1, N_MOD, D_MODEL), lambda i: (0, 0, 0)),
                  pl.BlockSpec((1, D_MODEL), lambda i: (0, 0)),
                  pl.BlockSpec((D_MODEL, 2 * KV_WIDTH), lambda i: (0, 0)),
                  pl.BlockSpec((1, HEAD_DIM), lambda i: (0, 0))],
        out_specs=[kv_spec, kv_spec],
        out_shape=[kv_shape, kv_shape],
        compiler_params=_params("parallel"),
        name="ctx_kv",
    )(ctx, mod_ctx, g_pre, w_kv, g_k)


def _pool_fold_kernel(wg_ref, ps_ref, wu_ref, o_ref):
    for g in range(len(POOL_WINDOWS)):
        sl = slice(g * POOL_GROUP_WIDTH, (g + 1) * POOL_GROUP_WIDTH)
        o_ref[sl, :] = jnp.dot(wg_ref[g] * ps_ref[:, sl], wu_ref[sl, :], precision=lax.Precision.HIGHEST,
                               preferred_element_type=F32).astype(BF16)


def _pool_fold(w_grp, pool_scale, w_pool_up):
    return pl.pallas_call(
        _pool_fold_kernel,
        out_shape=jax.ShapeDtypeStruct(w_pool_up.shape, BF16),
        compiler_params=pltpu.CompilerParams(vmem_limit_bytes=VMEM_LIMIT_BYTES),
        name="pool_fold",
    )(w_grp, pool_scale, w_pool_up)


def _rope_fold_kernel(cos_ref, sin_ref, gq_ref, gk_ref, cq_ref, sq_ref, ck_ref, sk_ref):
    cos = cos_ref[...]
    sin = sin_ref[...]
    for g_ref, c_ref, s_ref, logit_scale in ((gq_ref, cq_ref, sq_ref, Q_SCALE), (gk_ref, ck_ref, sk_ref, 1.0)):
        g = jnp.broadcast_to(g_ref[...], (F32_SUBLANES, HEAD_DIM)) * (logit_scale * math.sqrt(HEAD_DIM))
        g_partner = pltpu.roll(g, ROPE_PARTNER, 1)
        c_ref[...] = cos * g[0:1, :]
        s_ref[...] = sin * g_partner[0:1, :]


def _rope_fold(cos_t, sin_t, g_q, g_k):
    tab = jax.ShapeDtypeStruct(cos_t.shape, F32)
    return pl.pallas_call(
        _rope_fold_kernel,
        out_shape=[tab] * 4,
        compiler_params=pltpu.CompilerParams(vmem_limit_bytes=VMEM_LIMIT_BYTES),
        name="rope_fold",
    )(cos_t, sin_t, g_q, g_k)


def _in_proj_kernel(x_ref, xp_ref, xn_ref, mod_ref, gpre_ref, w_ref, bg_ref,
                    cq_ref, sq_ref, ck_ref, sk_ref, q_ref, k_ref, v_ref, pool_ref, gate_ref,
                    *, tm, sub, n_tiles, seq):
    i = pl.program_id(1)
    gpre = gpre_ref[...]
    shift = mod_ref[0, 0:1, :]
    scale = mod_ref[0, 1:2, :]
    subs = [slice(s * sub, (s + 1) * sub) for s in range(tm // sub)]
    hb = [_norm_mod(x_ref[0, rs, :], gpre, shift, scale).astype(BF16) for rs in subs]

    w_pool = w_ref[:, POOL_OFF:POOL_OFF + POOL_WIDTH]
    heads_per_dot = MXU_WIDTH // HEAD_DIM
    gate_chunk = GATE_WIDTH // (ATTN_WIDTH // MXU_WIDTH)

    def norm_rope(t, cos_g, sin_g):
        inv = lax.rsqrt(jnp.sum(t * t, axis=-1, keepdims=True) + HEAD_DIM * EPS)
        return ((t * cos_g + pltpu.roll(t, ROPE_PARTNER, 1) * sin_g) * inv).astype(BF16)

    def pool_windows(u_sub):
        u_main = jnp.concatenate(u_sub, axis=0)
        x_halo = jnp.concatenate([xp_ref[0], xn_ref[0]], axis=0)
        u_halo = _dot(_norm_mod(x_halo, gpre, shift, scale).astype(BF16), w_pool)
        u_prev = jnp.where(i > 0, u_halo[:POOL_HALO], 0.0)
        u_next = jnp.where(i < n_tiles - 1, u_halo[POOL_HALO:], 0.0)
        u_ext = jnp.concatenate([u_prev, u_main, u_next], axis=0)
        n_ext = tm + 2 * POOL_HALO
        edge_iota = lax.broadcasted_iota(jnp.int32, (POOL_HALO, POOL_GROUP_WIDTH), 0)
        edge_t = (i * tm + edge_iota, i * tm + (tm - POOL_HALO) + edge_iota)
        for gi, win in enumerate(POOL_WINDOWS):
            sl = slice(gi * POOL_GROUP_WIDTH, (gi + 1) * POOL_GROUP_WIDTH)
            half = win // 2
            run = u_ext[:, sl]
            length = 1
            while length < half:
                run = run + pltpu.roll(run, n_ext - length, 0)
                length *= 2
            window = run + pltpu.roll(run, half, 0)
            window = window[POOL_HALO:POOL_HALO + tm]
            inv_cnt = [1.0 / (jnp.minimum(t + half, seq) - jnp.maximum(t - half, 0)).astype(F32)
                       for t in edge_t]
            inv_cnt = jnp.concatenate(
                [inv_cnt[0], jnp.full((tm - 2 * POOL_HALO, POOL_GROUP_WIDTH), 1.0 / win, F32), inv_cnt[1]],
                axis=0)
            pool_ref[0, :, sl] = (window * inv_cnt - u_main[:, sl]).astype(BF16)

    u_sub = []
    for si, (rs, h) in enumerate(zip(subs, hb)):
        u_sub.append(_dot(h, w_pool))
        ts = pl.ds(pl.multiple_of(i * tm + si * sub, sub), sub)
        cq, sq = cq_ref[ts, :], sq_ref[ts, :]
        for j in range(ATTN_WIDTH // MXU_WIDTH):
            pq = _dot(h, w_ref[:, j * MXU_WIDTH:(j + 1) * MXU_WIDTH])
            for e in range(heads_per_dot):
                t = pq[:, e * HEAD_DIM:(e + 1) * HEAD_DIM]
                q_ref[0, j * heads_per_dot + e, rs, :] = norm_rope(t, cq, sq)
        pk = _dot(h, w_ref[:, K_OFF:K_OFF + KV_WIDTH])
        pv = _dot(h, w_ref[:, V_OFF:V_OFF + KV_WIDTH])
        for e in range(N_KV_HEADS):
            sl = slice(e * HEAD_DIM, (e + 1) * HEAD_DIM)
            k_ref[0, e, rs, :] = norm_rope(pk[:, sl], ck_ref[ts, :], sk_ref[ts, :])
            v_ref[0, e, rs, :] = pv[:, sl].astype(BF16)
    pool_windows(u_sub)
    for rs, h in zip(subs, hb):
        for j in range(GATE_WIDTH // gate_chunk):
            sl = slice(j * gate_chunk, (j + 1) * gate_chunk)
            logits = _dot(h, w_ref[:, GATE_OFF + j * gate_chunk:GATE_OFF + (j + 1) * gate_chunk])
            gate_ref[0, rs, sl] = jax.nn.sigmoid(logits + bg_ref[:, sl]).astype(BF16)


def _in_proj(x, mod, g_pre, w_in, b_gate, rope_tabs, *, tm, sub):
    b, s, _ = x.shape
    n_tiles = s // tm
    halo_blocks_per_tile = tm // POOL_HALO
    last_halo_block = s // POOL_HALO - 1
    const2 = lambda bi, i: (0, 0)
    in_specs = [
        pl.BlockSpec((1, tm, D_MODEL), lambda bi, i: (bi, i, 0)),
        pl.BlockSpec((1, POOL_HALO, D_MODEL),
                     lambda bi, i: (bi, jnp.maximum(i * halo_blocks_per_tile - 1, 0), 0)),
        pl.BlockSpec((1, POOL_HALO, D_MODEL),
                     lambda bi, i: (bi, jnp.minimum((i + 1) * halo_blocks_per_tile, last_halo_block), 0)),
        pl.BlockSpec((1, N_MOD, D_MODEL), lambda bi, i: (bi, 0, 0)),
        pl.BlockSpec((1, D_MODEL), const2),
        _VMEM_RESIDENT,
        pl.BlockSpec((1, GATE_WIDTH), const2),
    ] + [_VMEM_RESIDENT] * len(rope_tabs)
    out_specs = [
        pl.BlockSpec((1, N_Q_HEADS, tm, HEAD_DIM), lambda bi, i: (bi, 0, i, 0)),
        pl.BlockSpec((1, N_KV_HEADS, tm, HEAD_DIM), lambda bi, i: (bi, 0, i, 0)),
        pl.BlockSpec((1, N_KV_HEADS, tm, HEAD_DIM), lambda bi, i: (bi, 0, i, 0)),
        pl.BlockSpec((1, tm, POOL_WIDTH), lambda bi, i: (bi, i, 0)),
        pl.BlockSpec((1, tm, GATE_WIDTH), lambda bi, i: (bi, i, 0)),
    ]
    out_shape = [
        jax.ShapeDtypeStruct((b, N_Q_HEADS, s, HEAD_DIM), BF16),
        jax.ShapeDtypeStruct((b, N_KV_HEADS, s, HEAD_DIM), BF16),
        jax.ShapeDtypeStruct((b, N_KV_HEADS, s, HEAD_DIM), BF16),
        jax.ShapeDtypeStruct((b, s, POOL_WIDTH), BF16),
        jax.ShapeDtypeStruct((b, s, GATE_WIDTH), BF16),
    ]
    return pl.pallas_call(
        functools.partial(_in_proj_kernel, tm=tm, sub=sub, n_tiles=n_tiles, seq=s),
        grid=(b, n_tiles),
        in_specs=in_specs,
        out_specs=out_specs,
        out_shape=out_shape,
        compiler_params=_params("parallel", "parallel"),
        name="in_proj",
    )(x, x, x, mod, g_pre, w_in, b_gate, *rope_tabs)


def _attention_kernel(q_ref, kc_ref, vc_ref, kl_ref, vl_ref, o_ref, *, tq, rows):
    kc = kc_ref[0, 0]
    kl = kl_ref[0, 0]
    ones_cols = lambda n: jnp.ones((n, MXU_WIDTH - HEAD_DIM), BF16)
    vc = jnp.concatenate([vc_ref[0, 0], ones_cols(kc.shape[0])], axis=-1)
    vl = jnp.concatenate([vl_ref[0, 0], ones_cols(kl.shape[0])], axis=-1)
    for h in range(Q_GROUP):
        for r in range(tq // rows):
            rs = slice(r * rows, (r + 1) * rows)
            q = q_ref[0, 0, h, rs, :]
            s_c = _dot_nt(q, kc)
            s_l = _dot_nt(q, kl)
            m = jnp.maximum(jnp.max(s_c, axis=-1, keepdims=True),
                            jnp.max(s_l, axis=-1, keepdims=True))
            p_c = jnp.exp2(s_c - m).astype(BF16)
            p_l = jnp.exp2(s_l - m).astype(BF16)
            o = _dot(p_c, vc) + _dot(p_l, vl)
            o_ref[0, rs, h * HEAD_DIM:(h + 1) * HEAD_DIM] = (
                o[:, :HEAD_DIM] / o[:, HEAD_DIM:]).astype(BF16)


def _attention(q, k_ctx, v_ctx, k_lat, v_lat, *, tq, rows):
    b, _, s, _ = q.shape
    n_ctx = k_ctx.shape[2]
    qg = q.reshape(b, N_KV_HEADS, Q_GROUP, s, HEAD_DIM)
    kv_idx = lambda bi, kh, i: (bi, kh, 0, 0)
    return pl.pallas_call(
        functools.partial(_attention_kernel, tq=tq, rows=rows),
        grid=(b, N_KV_HEADS, s // tq),
        in_specs=[pl.BlockSpec((1, 1, Q_GROUP, tq, HEAD_DIM), lambda bi, kh, i: (bi, kh, 0, i, 0)),
                  pl.BlockSpec((1, 1, n_ctx, HEAD_DIM), kv_idx),
                  pl.BlockSpec((1, 1, n_ctx, HEAD_DIM), kv_idx),
                  pl.BlockSpec((1, 1, s, HEAD_DIM), kv_idx),
                  pl.BlockSpec((1, 1, s, HEAD_DIM), kv_idx)],
        out_specs=pl.BlockSpec((1, tq, Q_GROUP * HEAD_DIM), lambda bi, kh, i: (bi, i, kh)),
        out_shape=jax.ShapeDtypeStruct((b, s, ATTN_WIDTH), BF16),
        compiler_params=_params("parallel", "parallel", "parallel"),
        name="attention",
    )(qg, k_ctx, v_ctx, k_lat, v_lat)


def _merge_mlp_kernel(x_ref, a_ref, p_ref, g_ref, mod_ref, gmix_ref, gpre_ref, gpost_ref,
                      wa_ref, wp_ref, wo_ref, w1_ref, w2_ref, o_ref, *, tm, sub, ff_chunk):
    subs = [slice(s * sub, (s + 1) * sub) for s in range(tm // sub)]
    x_mid = []
    for rs in subs:
        ya = _dot(a_ref[0, rs, :], wa_ref[...])
        yp = _dot(p_ref[0, rs, :], wp_ref[...])
        ga = g_ref[0, rs, :D_MODEL].astype(F32)
        gp = g_ref[0, rs, D_MODEL:].astype(F32)
        y = _dot((ga * ya + gp * yp).astype(BF16), wo_ref[...])
        x_mid.append(x_ref[0, rs, :] + mod_ref[0, 2:3, :] * _rms(y, gmix_ref[...]))
    for rs, xs in zip(subs, x_mid):
        h = _norm_mod(xs, gpre_ref[...], mod_ref[0, 3:4, :], mod_ref[0, 4:5, :]).astype(BF16)
        acc = None
        for j in range(D_FF // ff_chunk):
            sl = slice(j * ff_chunk, (j + 1) * ff_chunk)
            t = jnp.maximum(_dot(h, w1_ref[:, sl]), 0.0)
            part = _dot((t * t).astype(BF16), w2_ref[sl, :])
            acc = part if acc is None else acc + part
        o_ref[0, rs, :] = xs + mod_ref[0, 5:6, :] * _rms(acc, gpost_ref[...])


def _merge_mlp(x, attn_o, pooled, gates, mod, g_post_mix, g_pre_mlp, g_post_mlp,
               w_attn_up, w_pool_fold, w_out, w_ff1, w_ff2, *, tm, sub, ff_chunk):
    b, s, _ = x.shape
    tile = lambda width: pl.BlockSpec((1, tm, width), lambda bi, i: (bi, i, 0))
    const2 = lambda bi, i: (0, 0)
    gain = pl.BlockSpec((1, D_MODEL), const2)
    return pl.pallas_call(
        functools.partial(_merge_mlp_kernel, tm=tm, sub=sub, ff_chunk=ff_chunk),
        grid=(b, s // tm),
        in_specs=[tile(D_MODEL), tile(ATTN_WIDTH), tile(POOL_WIDTH), tile(GATE_WIDTH),
                  pl.BlockSpec((1, N_MOD, D_MODEL), lambda bi, i: (bi, 0, 0)),
                  gain, gain, gain] + [_VMEM_RESIDENT] * 5,
        out_specs=tile(D_MODEL),
        out_shape=jax.ShapeDtypeStruct(x.shape, F32),
        compiler_params=_params("parallel", "parallel"),
        name="merge_mlp",
    )(x, attn_o, pooled, gates, mod, g_post_mix, g_pre_mlp, g_post_mlp,
      w_attn_up, w_pool_fold, w_out, w_ff1, w_ff2)


def _rope_tables(seq):
    t = np.arange(seq)
    rows = (t // GRID_W).astype(np.float64)
    cols = (t % GRID_W).astype(np.float64)
    freqs = ROPE_THETA ** (-np.arange(ROPE_FREQS, dtype=np.float64) / ROPE_FREQS)
    ang_row = rows[:, None] * freqs
    ang_col = cols[:, None] * freqs
    cos_r, sin_r, cos_c, sin_c = np.cos(ang_row), np.sin(ang_row), np.cos(ang_col), np.sin(ang_col)
    cos_t = np.concatenate([cos_r, cos_c, cos_r, cos_c], axis=-1).astype(np.float32)
    sin_t = np.concatenate([-sin_r, -sin_c, sin_r, sin_c], axis=-1).astype(np.float32)
    return jnp.asarray(cos_t), jnp.asarray(sin_t)


def kernel(x, c, ctx, c_ctx, w_mod, b_mod, g_pre_mix, g_post_mix, g_pre_mlp, g_post_mlp, w_in, b_gate, g_q, g_k, w_attn_up, w_pool_grp, pool_scale, w_pool_up, w_out, w_ff1, w_ff2):
    depth = w_mod.shape[0]
    assert depth == 1, "single-layer block"
    b, s, _ = x.shape
    row = lambda a: a[0].reshape(1, -1)

    n_cond = -(-(b + 1) // F32_SUBLANES) * F32_SUBLANES
    cond = jnp.concatenate([c, c_ctx[None, :], jnp.zeros((n_cond - b - 1, D_MODEL), F32)], axis=0)
    mod_all = _adaln(cond, w_mod[0], b_mod[0])
    mod = mod_all[:b].reshape(b, N_MOD, D_MODEL)
    mod_ctx = mod_all[b:b + 1].reshape(1, N_MOD, D_MODEL)

    w_in_b = _w_in_prep(w_in[0])
    g_q_p = _permute_head_dim(row(g_q))
    g_k_p = _permute_head_dim(row(g_k))
    cos_t, sin_t = _rope_tables(s)

    k_ctx, v_ctx = _ctx_kv(ctx, mod_ctx, row(g_pre_mix), w_in_b[:, K_OFF:POOL_OFF], g_k_p, samples=8)
    rope_tabs = _rope_fold(cos_t, sin_t, g_q_p, g_k_p)
    q, k_lat, v_lat, pooled, gates = _in_proj(
        x, mod, row(g_pre_mix), w_in_b, row(b_gate), rope_tabs, tm=1024, sub=256)
    attn_o = _attention(q, k_ctx, v_ctx, k_lat, v_lat, tq=2048, rows=128)
    w_pool_fold = _pool_fold(w_pool_grp[0], row(pool_scale), w_pool_up[0])
    return _merge_mlp(x, attn_o, pooled, gates, mod, row(g_post_mix), row(g_pre_mlp), row(g_post_mlp),
                      w_attn_up[0].astype(BF16), w_pool_fold, w_out[0].astype(BF16),
                      w_ff1[0].astype(BF16), w_ff2[0].astype(BF16), tm=1024, sub=256, ff_chunk=1024)
```

```python
import functools
import math

import jax
import jax.numpy as jnp
import numpy as np
from jax import lax
from jax.experimental import pallas as pl
from jax.experimental.pallas import tpu as pltpu

D_MODEL = 1024
GRID_W = 64
HEAD_DIM = 128
N_Q_HEADS = 8
N_KV_HEADS = 2
Q_GROUP = N_Q_HEADS // N_KV_HEADS
ATTN_WIDTH = N_Q_HEADS * HEAD_DIM
KV_WIDTH = N_KV_HEADS * HEAD_DIM
POOL_WINDOWS = (2, 4, 8, 16)
POOL_WIDTH = 512
POOL_GROUP_WIDTH = POOL_WIDTH // len(POOL_WINDOWS)
GATE_WIDTH = 2 * D_MODEL
K_OFF = ATTN_WIDTH
V_OFF = K_OFF + KV_WIDTH
POOL_OFF = V_OFF + KV_WIDTH
GATE_OFF = POOL_OFF + POOL_WIDTH
D_FF = 4 * D_MODEL
ROPE_THETA = 10000.0
ROPE_FREQS = HEAD_DIM // 4
ROPE_PARTNER = HEAD_DIM // 2
EPS = 1e-6
N_MOD = 6

F32_SUBLANES = 8
MXU_WIDTH = 256
VMEM_LIMIT_BYTES = 62 * 1024 * 1024

POOL_HALO = max(POOL_WINDOWS) // 2
assert POOL_HALO == F32_SUBLANES

Q_SCALE = (HEAD_DIM ** -0.5) * math.log2(math.e)

BF16 = jnp.bfloat16
F32 = jnp.float32


def _dot(a, b):
    return jnp.dot(a, b, preferred_element_type=F32)


def _dot_nt(a, b):
    return lax.dot_general(a, b, (((1,), (1,)), ((), ())), preferred_element_type=F32)


def _rms(x, gain):
    ms = jnp.mean(x * x, axis=-1, keepdims=True)
    return x * lax.rsqrt(ms + EPS) * gain


def _norm_mod(x, gain, shift, scale):
    return _rms(x, gain * (1.0 + scale)) + shift


def _permute_head_dim(a):
    lead = a.shape[:-1]
    blocks = a.reshape(*lead, -1, 2, 2, ROPE_FREQS)
    return jnp.swapaxes(blocks, -3, -2).reshape(a.shape)


_VMEM_RESIDENT = pl.BlockSpec(memory_space=pltpu.VMEM)


def _params(*sem):
    return pltpu.CompilerParams(dimension_semantics=sem, vmem_limit_bytes=VMEM_LIMIT_BYTES)


def _w_in_prep_kernel(w_ref, o_ref, *, bn):
    is_qk = pl.program_id(0) * bn < V_OFF

    @pl.when(is_qk)
    def _():
        block = lax.broadcasted_iota(jnp.int32, (w_ref.shape[0], HEAD_DIM), 1) // ROPE_FREQS
        for e in range(bn // HEAD_DIM):
            sl = slice(e * HEAD_DIM, (e + 1) * HEAD_DIM)
            t = w_ref[:, sl]
            swapped = jnp.where(block == 1, pltpu.roll(t, HEAD_DIM - ROPE_FREQS, 1),
                                jnp.where(block == 2, pltpu.roll(t, ROPE_FREQS, 1), t))
            o_ref[:, sl] = swapped.astype(BF16)

    @pl.when(jnp.logical_not(is_qk))
    def _():
        o_ref[...] = w_ref[...].astype(BF16)


def _w_in_prep(w_in):
    k, n = w_in.shape
    bn = MXU_WIDTH
    assert V_OFF % bn == 0
    return pl.pallas_call(
        functools.partial(_w_in_prep_kernel, bn=bn),
        grid=(n // bn,),
        in_specs=[pl.BlockSpec((k, bn), lambda j: (0, j))],
        out_specs=pl.BlockSpec((k, bn), lambda j: (0, j)),
        out_shape=jax.ShapeDtypeStruct((k, n), BF16),
        compiler_params=_params("parallel"),
        name="w_in_prep",
    )(w_in)


def _adaln_kernel(c_ref, w_ref, b_ref, o_ref):
    c = c_ref[...]
    a = (c * jax.nn.sigmoid(c)).astype(BF16)
    o_ref[...] = _dot(a, w_ref[...].astype(BF16)) + b_ref[...]


def _adaln(cond, w_mod, b_mod):
    rows = cond.shape[0]
    n_out = w_mod.shape[1]
    bn = D_MODEL
    return pl.pallas_call(
        _adaln_kernel,
        grid=(n_out // bn,),
        in_specs=[pl.BlockSpec((rows, D_MODEL), lambda j: (0, 0)),
                  pl.BlockSpec((D_MODEL, bn), lambda j: (0, j)),
                  pl.BlockSpec((1, bn), lambda j: (0, j))],
        out_specs=pl.BlockSpec((rows, bn), lambda j: (0, j)),
        out_shape=jax.ShapeDtypeStruct((rows, n_out), F32),
        compiler_params=_params("parallel"),
        name="adaln",
    )(cond, w_mod, b_mod.reshape(1, n_out))


def _ctx_kv_kernel(x_ref, mod_ref, gpre_ref, w_ref, gk_ref, k_ref, v_ref):
    for s in range(x_ref.shape[0]):
        h = _norm_mod(x_ref[s], gpre_ref[...], mod_ref[0, 0:1, :], mod_ref[0, 1:2, :]).astype(BF16)
        pk = _dot(h, w_ref[:, :KV_WIDTH])
        pv = _dot(h, w_ref[:, KV_WIDTH:])
        for e in range(N_KV_HEADS):
            sl = slice(e * HEAD_DIM, (e + 1) * HEAD_DIM)
            k_ref[s, e] = _rms(pk[:, sl], gk_ref[...]).astype(BF16)
            v_ref[s, e] = pv[:, sl].astype(BF16)


def _ctx_kv(ctx, mod_ctx, g_pre, w_kv, g_k, *, samples):
    b, n, _ = ctx.shape
    kv_shape = jax.ShapeDtypeStruct((b, N_KV_HEADS, n, HEAD_DIM), BF16)
    kv_spec = pl.BlockSpec((samples, N_KV_HEADS, n, HEAD_DIM), lambda i: (i, 0, 0, 0))
    return pl.pallas_call(
        _ctx_kv_kernel,
        grid=(b // samples,),
        in_specs=[pl.BlockSpec((samples, n, D_MODEL), lambda i: (i, 0, 0)),
                  pl.BlockSpec((1, N_MOD, D_MODEL), lambda i: (0, 0, 0)),
                  pl.BlockSpec((1, D_MODEL), lambda i: (0, 0)),
                  pl.BlockSpec((D_MODEL, 2 * KV_WIDTH), lambda i: (0, 0)),
                  pl.BlockSpec((1, HEAD_DIM), lambda i: (0, 0))],
        out_specs=[kv_spec, kv_spec],
        out_shape=[kv_shape, kv_shape],
        compiler_params=_params("parallel"),
        name="ctx_kv",
    )(ctx, mod_ctx, g_pre, w_kv, g_k)


def _pool_fold_kernel(wg_ref, ps_ref, wu_ref, o_ref):
    for g in range(len(POOL_WINDOWS)):
        sl = slice(g * POOL_GROUP_WIDTH, (g + 1) * POOL_GROUP_WIDTH)
        o_ref[sl, :] = jnp.dot(wg_ref[g] * ps_ref[:, sl], wu_ref[sl, :], precision=lax.Precision.HIGHEST,
                               preferred_element_type=F32).astype(BF16)


def _pool_fold(w_grp, pool_scale, w_pool_up):
    return pl.pallas_call(
        _pool_fold_kernel,
        out_shape=jax.ShapeDtypeStruct(w_pool_up.shape, BF16),
        compiler_params=pltpu.CompilerParams(vmem_limit_bytes=VMEM_LIMIT_BYTES),
        name="pool_fold",
    )(w_grp, pool_scale, w_pool_up)


def _rope_fold_kernel(cos_ref, sin_ref, gq_ref, gk_ref, cq_ref, sq_ref, ck_ref, sk_ref):
    cos = cos_ref[...]
    sin = sin_ref[...]
    for g_ref, c_ref, s_ref, logit_scale in ((gq_ref, cq_ref, sq_ref, Q_SCALE), (gk_ref, ck_ref, sk_ref, 1.0)):
        g = jnp.broadcast_to(g_ref[...], (F32_SUBLANES, HEAD_DIM)) * (logit_scale * math.sqrt(HEAD_DIM))
        g_partner = pltpu.roll(g, ROPE_PARTNER, 1)
        c_ref[...] = cos * g[0:1, :]
        s_ref[...] = sin * g_partner[0:1, :]


def _rope_fold(cos_t, sin_t, g_q, g_k):
    tab = jax.ShapeDtypeStruct(cos_t.shape, F32)
    return pl.pallas_call(
        _rope_fold_kernel,
        out_shape=[tab] * 4,
        compiler_params=pltpu.CompilerParams(vmem_limit_bytes=VMEM_LIMIT_BYTES),
        name="rope_fold",
    )(cos_t, sin_t, g_q, g_k)


def _in_proj_kernel(x_ref, xp_ref, xn_ref, mod_ref, gpre_ref, w_ref, bg_ref,
                    cq_ref, sq_ref, ck_ref, sk_ref, q_ref, k_ref, v_ref, pool_ref, gate_ref,
                    *, tm, sub, n_tiles, seq):
    i = pl.program_id(1)
    gpre = gpre_ref[...]
    shift = mod_ref[0, 0:1, :]
    scale = mod_ref[0, 1:2, :]
    subs = [slice(s * sub, (s + 1) * sub) for s in range(tm // sub)]
    hb = [_norm_mod(x_ref[0, rs, :], gpre, shift, scale).astype(BF16) for rs in subs]

    w_pool = w_ref[:, POOL_OFF:POOL_OFF + POOL_WIDTH]
    heads_per_dot = MXU_WIDTH // HEAD_DIM
    gate_chunk = GATE_WIDTH // (ATTN_WIDTH // MXU_WIDTH)

    def norm_rope(t, cos_g, sin_g):
        inv = lax.rsqrt(jnp.sum(t * t, axis=-1, keepdims=True) + HEAD_DIM * EPS)
        return ((t * cos_g + pltpu.roll(t, ROPE_PARTNER, 1) * sin_g) * inv).astype(BF16)

    def pool_windows(u_sub):
        u_main = jnp.concatenate(u_sub, axis=0)
        x_halo = jnp.concatenate([xp_ref[0], xn_ref[0]], axis=0)
        u_halo = _dot(_norm_mod(x_halo, gpre, shift, scale).astype(BF16), w_pool)
        u_prev = jnp.where(i > 0, u_halo[:POOL_HALO], 0.0)
        u_next = jnp.where(i < n_tiles - 1, u_halo[POOL_HALO:], 0.0)
        u_ext = jnp.concatenate([u_prev, u_main, u_next], axis=0)
        n_ext = tm + 2 * POOL_HALO
        edge_iota = lax.broadcasted_iota(jnp.int32, (POOL_HALO, POOL_GROUP_WIDTH), 0)
        edge_t = (i * tm + edge_iota, i * tm + (tm - POOL_HALO) + edge_iota)
        for gi, win in enumerate(POOL_WINDOWS):
            sl = slice(gi * POOL_GROUP_WIDTH, (gi + 1) * POOL_GROUP_WIDTH)
            half = win // 2
            run = u_ext[:, sl]
            length = 1
            while length < half:
                run = run + pltpu.roll(run, n_ext - length, 0)
                length *= 2
            window = run + pltpu.roll(run, half, 0)
            window = window[POOL_HALO:POOL_HALO + tm]
            inv_cnt = [1.0 / (jnp.minimum(t + half, seq) - jnp.maximum(t - half, 0)).astype(F32)
                       for t in edge_t]
            inv_cnt = jnp.concatenate(
                [inv_cnt[0], jnp.full((tm - 2 * POOL_HALO, POOL_GROUP_WIDTH), 1.0 / win, F32), inv_cnt[1]],
                axis=0)
            pool_ref[0, :, sl] = (window * inv_cnt - u_main[:, sl]).astype(BF16)

    u_sub = []
    for si, (rs, h) in enumerate(zip(subs, hb)):
        u_sub.append(_dot(h, w_pool))
        ts = pl.ds(pl.multiple_of(i * tm + si * sub, sub), sub)
        cq, sq = cq_ref[ts, :], sq_ref[ts, :]
        for j in range(ATTN_WIDTH // MXU_WIDTH):
            pq = _dot(h, w_ref[:, j * MXU_WIDTH:(j + 1) * MXU_WIDTH])
            for e in range(heads_per_dot):
                t = pq[:, e * HEAD_DIM:(e + 1) * HEAD_DIM]
                q_ref[0, j * heads_per_dot + e, rs, :] = norm_rope(t, cq, sq)
        pk = _dot(h, w_ref[:, K_OFF:K_OFF + KV_WIDTH])
        pv = _dot(h, w_ref[:, V_OFF:V_OFF + KV_WIDTH])
        for e in range(N_KV_HEADS):
            sl = slice(e * HEAD_DIM, (e + 1) * HEAD_DIM)
            k_ref[0, e, rs, :] = norm_rope(pk[:, sl], ck_ref[ts, :], sk_ref[ts, :])
            v_ref[0, e, rs, :] = pv[:, sl].astype(BF16)
    pool_windows(u_sub)
    for rs, h in zip(subs, hb):
        for j in range(GATE_WIDTH // gate_chunk):
            sl = slice(j * gate_chunk, (j + 1) * gate_chunk)
            logits = _dot(h, w_ref[:, GATE_OFF + j * gate_chunk:GATE_OFF + (j + 1) * gate_chunk])
            gate_ref[0, rs, sl] = jax.nn.sigmoid(logits + bg_ref[:, sl]).astype(BF16)


def _in_proj(x, mod, g_pre, w_in, b_gate, rope_tabs, *, tm, sub):
    b, s, _ = x.shape
    n_tiles = s // tm
    halo_blocks_per_tile = tm // POOL_HALO
    last_halo_block = s // POOL_HALO - 1
    const2 = lambda bi, i: (0, 0)
    in_specs = [
        pl.BlockSpec((1, tm, D_MODEL), lambda bi, i: (bi, i, 0)),
        pl.BlockSpec((1, POOL_HALO, D_MODEL),
                     lambda bi, i: (bi, jnp.maximum(i * halo_blocks_per_tile - 1, 0), 0)),
        pl.BlockSpec((1, POOL_HALO, D_MODEL),
                     lambda bi, i: (bi, jnp.minimum((i + 1) * halo_blocks_per_tile, last_halo_block), 0)),
        pl.BlockSpec((1, N_MOD, D_MODEL), lambda bi, i: (bi, 0, 0)),
        pl.BlockSpec((1, D_MODEL), const2),
        _VMEM_RESIDENT,
        pl.BlockSpec((1, GATE_WIDTH), const2),
    ] + [_VMEM_RESIDENT] * len(rope_tabs)
    out_specs = [
        pl.BlockSpec((1, N_Q_HEADS, tm, HEAD_DIM), lambda bi, i: (bi, 0, i, 0)),
        pl.BlockSpec((1, N_KV_HEADS, tm, HEAD_DIM), lambda bi, i: (bi, 0, i, 0)),
        pl.BlockSpec((1, N_KV_HEADS, tm, HEAD_DIM), lambda bi, i: (bi, 0, i, 0)),
        pl.BlockSpec((1, tm, POOL_WIDTH), lambda bi, i: (bi, i, 0)),
        pl.BlockSpec((1, tm, GATE_WIDTH), lambda bi, i: (bi, i, 0)),
    ]
    out_shape = [
        jax.ShapeDtypeStruct((b, N_Q_HEADS, s, HEAD_DIM), BF16),
        jax.ShapeDtypeStruct((b, N_KV_HEADS, s, HEAD_DIM), BF16),
        jax.ShapeDtypeStruct((b, N_KV_HEADS, s, HEAD_DIM), BF16),
        jax.ShapeDtypeStruct((b, s, POOL_WIDTH), BF16),
        jax.ShapeDtypeStruct((b, s, GATE_WIDTH), BF16),
    ]
    return pl.pallas_call(
        functools.partial(_in_proj_kernel, tm=tm, sub=sub, n_tiles=n_tiles, seq=s),
        grid=(b, n_tiles),
        in_specs=in_specs,
        out_specs=out_specs,
        out_shape=out_shape,
        compiler_params=_params("parallel", "parallel"),
        name="in_proj",
    )(x, x, x, mod, g_pre, w_in, b_gate, *rope_tabs)


def _attention_kernel(q_ref, kc_ref, vc_ref, kl_ref, vl_ref, o_ref, *, tq, rows):
    kc = kc_ref[0, 0]
    kl = kl_ref[0, 0]
    ones_cols = lambda n: jnp.ones((n, MXU_WIDTH - HEAD_DIM), BF16)
    vc = jnp.concatenate([vc_ref[0, 0], ones_cols(kc.shape[0])], axis=-1)
    vl = jnp.concatenate([vl_ref[0, 0], ones_cols(kl.shape[0])], axis=-1)
    for h in range(Q_GROUP):
        for r in range(tq // rows):
            rs = slice(r * rows, (r + 1) * rows)
            q = q_ref[0, 0, h, rs, :]
            s_c = _dot_nt(q, kc)
            s_l = _dot_nt(q, kl)
            m = jnp.maximum(jnp.max(s_c, axis=-1, keepdims=True),
                            jnp.max(s_l, axis=-1, keepdims=True))
            p_c = jnp.exp2(s_c - m).astype(BF16)
            p_l = jnp.exp2(s_l - m).astype(BF16)
            o = _dot(p_c, vc) + _dot(p_l, vl)
            o_ref[0, rs, h * HEAD_DIM:(h + 1) * HEAD_DIM] = (
                o[:, :HEAD_DIM] / o[:, HEAD_DIM:]).astype(BF16)


def _attention(q, k_ctx, v_ctx, k_lat, v_lat, *, tq, rows):
    b, _, s, _ = q.shape
    n_ctx = k_ctx.shape[2]
    qg = q.reshape(b, N_KV_HEADS, Q_GROUP, s, HEAD_DIM)
    kv_idx = lambda bi, kh, i: (bi, kh, 0, 0)
    return pl.pallas_call(
        functools.partial(_attention_kernel, tq=tq, rows=rows),
        grid=(b, N_KV_HEADS, s // tq),
        in_specs=[pl.BlockSpec((1, 1, Q_GROUP, tq, HEAD_DIM), lambda bi, kh, i: (bi, kh, 0, i, 0)),
                  pl.BlockSpec((1, 1, n_ctx, HEAD_DIM), kv_idx),
                  pl.BlockSpec((1, 1, n_ctx, HEAD_DIM), kv_idx),
                  pl.BlockSpec((1, 1, s, HEAD_DIM), kv_idx),
                  pl.BlockSpec((1, 1, s, HEAD_DIM), kv_idx)],
        out_specs=pl.BlockSpec((1, tq, Q_GROUP * HEAD_DIM), lambda bi, kh, i: (bi, i, kh)),
        out_shape=jax.ShapeDtypeStruct((b, s, ATTN_WIDTH), BF16),
        compiler_params=_params("parallel", "parallel", "parallel"),
        name="attention",
    )(qg, k_ctx, v_ctx, k_lat, v_lat)


def _merge_mlp_kernel(x_ref, a_ref, p_ref, g_ref, mod_ref, gmix_ref, gpre_ref, gpost_ref,
                      wa_ref, wp_ref, wo_ref, w1_ref, w2_ref, o_ref, *, tm, sub, ff_chunk):
    subs = [slice(s * sub, (s + 1) * sub) for s in range(tm // sub)]
    x_mid = []
    for rs in subs:
        ya = _dot(a_ref[0, rs, :], wa_ref[...]).astype(BF16)
        yp = _dot(p_ref[0, rs, :], wp_ref[...]).astype(BF16)
        z = g_ref[0, rs, :D_MODEL] * ya + g_ref[0, rs, D_MODEL:] * yp
        y = _dot(z, wo_ref[...])
        x_mid.append(x_ref[0, rs, :] + _rms(y, gmix_ref[...] * mod_ref[0, 2:3, :]))
    for rs, xs in zip(subs, x_mid):
        h = _norm_mod(xs, gpre_ref[...], mod_ref[0, 3:4, :], mod_ref[0, 4:5, :]).astype(BF16)
        acc = None
        for j in range(D_FF // ff_chunk):
            sl = slice(j * ff_chunk, (j + 1) * ff_chunk)
            t = jnp.maximum(_dot(h, w1_ref[:, sl]), 0.0)
            part = _dot((t * t).astype(BF16), w2_ref[sl, :])
            acc = part if acc is None else acc + part
        o_ref[0, rs, :] = xs + _rms(acc, gpost_ref[...] * mod_ref[0, 5:6, :])


def _merge_mlp(x, attn_o, pooled, gates, mod, g_post_mix, g_pre_mlp, g_post_mlp,
               w_attn_up, w_pool_fold, w_out, w_ff1, w_ff2, *, tm, sub, ff_chunk):
    b, s, _ = x.shape
    tile = lambda width: pl.BlockSpec((1, tm, width), lambda bi, i: (bi, i, 0))
    const2 = lambda bi, i: (0, 0)
    gain = pl.BlockSpec((1, D_MODEL), const2)
    return pl.pallas_call(
        functools.partial(_merge_mlp_kernel, tm=tm, sub=sub, ff_chunk=ff_chunk),
        grid=(b, s // tm),
        in_specs=[tile(D_MODEL), tile(ATTN_WIDTH), tile(POOL_WIDTH), tile(GATE_WIDTH),
                  pl.BlockSpec((1, N_MOD, D_MODEL), lambda bi, i: (bi, 0, 0)),
                  gain, gain, gain] + [_VMEM_RESIDENT] * 5,
        out_specs=tile(D_MODEL),
        out_shape=jax.ShapeDtypeStruct(x.shape, F32),
        compiler_params=_params("parallel", "parallel"),
        name="merge_mlp",
    )(x, attn_o, pooled, gates, mod, g_post_mix, g_pre_mlp, g_post_mlp,
      w_attn_up, w_pool_fold, w_out, w_ff1, w_ff2)


def _rope_tables(seq):
    t = np.arange(seq)
    rows = (t // GRID_W).astype(np.float64)
    cols = (t % GRID_W).astype(np.float64)
    freqs = ROPE_THETA ** (-np.arange(ROPE_FREQS, dtype=np.float64) / ROPE_FREQS)
    ang_row = rows[:, None] * freqs
    ang_col = cols[:, None] * freqs
    cos_r, sin_r, cos_c, sin_c = np.cos(ang_row), np.sin(ang_row), np.cos(ang_col), np.sin(ang_col)
    cos_t = np.concatenate([cos_r, cos_c, cos_r, cos_c], axis=-1).astype(np.float32)
    sin_t = np.concatenate([-sin_r, -sin_c, sin_r, sin_c], axis=-1).astype(np.float32)
    return jnp.asarray(cos_t), jnp.asarray(sin_t)


def kernel(x, c, ctx, c_ctx, w_mod, b_mod, g_pre_mix, g_post_mix, g_pre_mlp, g_post_mlp, w_in, b_gate, g_q, g_k, w_attn_up, w_pool_grp, pool_scale, w_pool_up, w_out, w_ff1, w_ff2):
    depth = w_mod.shape[0]
    assert depth == 1, "single-layer block"
    b, s, _ = x.shape
    row = lambda a: a[0].reshape(1, -1)

    n_cond = -(-(b + 1) // F32_SUBLANES) * F32_SUBLANES
    cond = jnp.concatenate([c, c_ctx[None, :], jnp.zeros((n_cond - b - 1, D_MODEL), F32)], axis=0)
    mod_all = _adaln(cond, w_mod[0], b_mod[0])
    mod = mod_all[:b].reshape(b, N_MOD, D_MODEL)
    mod_ctx = mod_all[b:b + 1].reshape(1, N_MOD, D_MODEL)

    w_in_b = _w_in_prep(w_in[0])
    g_q_p = _permute_head_dim(row(g_q))
    g_k_p = _permute_head_dim(row(g_k))
    cos_t, sin_t = _rope_tables(s)

    k_ctx, v_ctx = _ctx_kv(ctx, mod_ctx, row(g_pre_mix), w_in_b[:, K_OFF:POOL_OFF], g_k_p, samples=8)
    rope_tabs = _rope_fold(cos_t, sin_t, g_q_p, g_k_p)
    q, k_lat, v_lat, pooled, gates = _in_proj(
        x, mod, row(g_pre_mix), w_in_b, row(b_gate), rope_tabs, tm=1024, sub=256)
    attn_o = _attention(q, k_ctx, v_ctx, k_lat, v_lat, tq=2048, rows=128)
    w_pool_fold = _pool_fold(w_pool_grp[0], row(pool_scale), w_pool_up[0])
    return _merge_mlp(x, attn_o, pooled, gates, mod, row(g_post_mix), row(g_pre_mlp), row(g_post_mlp),
                      w_attn_up[0].astype(BF16), w_pool_fold, w_out[0].astype(BF16),
                      w_ff1[0].astype(BF16), w_ff2[0].astype(BF16), tm=1024, sub=256, ff_chunk=1024)
```

```python
import functools
import math

import jax
import jax.numpy as jnp
import numpy as np
from jax import lax
from jax.experimental import pallas as pl
from jax.experimental.pallas import tpu as pltpu

D_MODEL = 1024
GRID_W = 64
HEAD_DIM = 128
N_Q_HEADS = 8
N_KV_HEADS = 2
Q_GROUP = N_Q_HEADS // N_KV_HEADS
ATTN_WIDTH = N_Q_HEADS * HEAD_DIM
KV_WIDTH = N_KV_HEADS * HEAD_DIM
POOL_WINDOWS = (2, 4, 8, 16)
POOL_WIDTH = 512
POOL_GROUP_WIDTH = POOL_WIDTH // len(POOL_WINDOWS)
GATE_WIDTH = 2 * D_MODEL
K_OFF = ATTN_WIDTH
V_OFF = K_OFF + KV_WIDTH
POOL_OFF = V_OFF + KV_WIDTH
GATE_OFF = POOL_OFF + POOL_WIDTH
D_FF = 4 * D_MODEL
ROPE_THETA = 10000.0
ROPE_FREQS = HEAD_DIM // 4
ROPE_PARTNER = HEAD_DIM // 2
EPS = 1e-6
N_MOD = 6

F32_SUBLANES = 8
MXU_WIDTH = 256
VMEM_LIMIT_BYTES = 62 * 1024 * 1024

POOL_HALO = max(POOL_WINDOWS) // 2
assert POOL_HALO == F32_SUBLANES

Q_SCALE = (HEAD_DIM ** -0.5) * math.log2(math.e)

BF16 = jnp.bfloat16
F32 = jnp.float32


def _dot(a, b):
    return jnp.dot(a, b, preferred_element_type=F32)


def _dot_nt(a, b):
    return lax.dot_general(a, b, (((1,), (1,)), ((), ())), preferred_element_type=F32)


def _rms(x, gain):
    ms = jnp.mean(x * x, axis=-1, keepdims=True)
    return x * lax.rsqrt(ms + EPS) * gain


def _norm_mod(x, gain, shift, scale):
    return _rms(x, gain * (1.0 + scale)) + shift


def _permute_head_dim(a):
    lead = a.shape[:-1]
    blocks = a.reshape(*lead, -1, 2, 2, ROPE_FREQS)
    return jnp.swapaxes(blocks, -3, -2).reshape(a.shape)


_VMEM_RESIDENT = pl.BlockSpec(memory_space=pltpu.VMEM)


def _params(*sem):
    return pltpu.CompilerParams(dimension_semantics=sem, vmem_limit_bytes=VMEM_LIMIT_BYTES)


def _w_in_prep_kernel(w_ref, o_ref, *, bn):
    is_qk = pl.program_id(0) * bn < V_OFF

    @pl.when(is_qk)
    def _():
        block = lax.broadcasted_iota(jnp.int32, (w_ref.shape[0], HEAD_DIM), 1) // ROPE_FREQS
        for e in range(bn // HEAD_DIM):
            sl = slice(e * HEAD_DIM, (e + 1) * HEAD_DIM)
            t = w_ref[:, sl]
            swapped = jnp.where(block == 1, pltpu.roll(t, HEAD_DIM - ROPE_FREQS, 1),
                                jnp.where(block == 2, pltpu.roll(t, ROPE_FREQS, 1), t))
            o_ref[:, sl] = swapped.astype(BF16)

    @pl.when(jnp.logical_not(is_qk))
    def _():
        col_scale = jnp.where(pl.program_id(0) * bn >= GATE_OFF, 0.5, 1.0)
        o_ref[...] = (w_ref[...] * col_scale).astype(BF16)


def _w_in_prep(w_in):
    k, n = w_in.shape
    bn = MXU_WIDTH
    assert V_OFF % bn == 0 and GATE_OFF % bn == 0
    return pl.pallas_call(
        functools.partial(_w_in_prep_kernel, bn=bn),
        grid=(n // bn,),
        in_specs=[pl.BlockSpec((k, bn), lambda j: (0, j))],
        out_specs=pl.BlockSpec((k, bn), lambda j: (0, j)),
        out_shape=jax.ShapeDtypeStruct((k, n), BF16),
        compiler_params=_params("parallel"),
        name="w_in_prep",
    )(w_in)


def _adaln_kernel(c_ref, w_ref, b_ref, o_ref):
    c = c_ref[...]
    a = (c * jax.nn.sigmoid(c)).astype(BF16)
    o_ref[...] = _dot(a, w_ref[...].astype(BF16)) + b_ref[...]


def _adaln(cond, w_mod, b_mod):
    rows = cond.shape[0]
    n_out = w_mod.shape[1]
    bn = D_MODEL
    return pl.pallas_call(
        _adaln_kernel,
        grid=(n_out // bn,),
        in_specs=[pl.BlockSpec((rows, D_MODEL), lambda j: (0, 0)),
                  pl.BlockSpec((D_MODEL, bn), lambda j: (0, j)),
                  pl.BlockSpec((1, bn), lambda j: (0, j))],
        out_specs=pl.BlockSpec((rows, bn), lambda j: (0, j)),
        out_shape=jax.ShapeDtypeStruct((rows, n_out), F32),
        compiler_params=_params("parallel"),
        name="adaln",
    )(cond, w_mod, b_mod.reshape(1, n_out))


def _ctx_kv_kernel(x_ref, mod_ref, gpre_ref, w_ref, gk_ref, k_ref, v_ref):
    for s in range(x_ref.shape[0]):
        h = _norm_mod(x_ref[s], gpre_ref[...], mod_ref[0, 0:1, :], mod_ref[0, 1:2, :]).astype(BF16)
        pk = _dot(h, w_ref[:, :KV_WIDTH])
        pv = _dot(h, w_ref[:, KV_WIDTH:])
        for e in range(N_KV_HEADS):
            sl = slice(e * HEAD_DIM, (e + 1) * HEAD_DIM)
            k_ref[s, e] = _rms(pk[:, sl], gk_ref[...]).astype(BF16)
            v_ref[s, e] = pv[:, sl].astype(BF16)


def _ctx_kv(ctx, mod_ctx, g_pre, w_kv, g_k, *, samples):
    b, n, _ = ctx.shape
    kv_shape = jax.ShapeDtypeStruct((b, N_KV_HEADS, n, HEAD_DIM), BF16)
    kv_spec = pl.BlockSpec((samples, N_KV_HEADS, n, HEAD_DIM), lambda i: (i, 0, 0, 0))
    return pl.pallas_call(
        _ctx_kv_kernel,
        grid=(b // samples,),
        in_specs=[pl.BlockSpec((samples, n, D_MODEL), lambda i: (i, 0, 0)),
                  pl.BlockSpec((1, N_MOD, D_MODEL), lambda i: (0, 0, 0)),
                  pl.BlockSpec((1, D_MODEL), lambda i: (0, 0)),
                  pl.BlockSpec((D_MODEL, 2 * KV_WIDTH), lambda i: (0, 0)),
                  pl.BlockSpec((1, HEAD_DIM), lambda i: (0, 0))],
        out_specs=[kv_spec, kv_spec],
        out_shape=[kv_shape, kv_shape],
        compiler_params=_params("parallel"),
        name="ctx_kv",
    )(ctx, mod_ctx, g_pre, w_kv, g_k)


def _pool_fold_kernel(wg_ref, ps_ref, wu_ref, o_ref):
    for g in range(len(POOL_WINDOWS)):
        sl = slice(g * POOL_GROUP_WIDTH, (g + 1) * POOL_GROUP_WIDTH)
        o_ref[sl, :] = jnp.dot(wg_ref[g] * ps_ref[:, sl], wu_ref[sl, :], precision=lax.Precision.HIGHEST,
                               preferred_element_type=F32).astype(BF16)


def _pool_fold(w_grp, pool_scale, w_pool_up):
    return pl.pallas_call(
        _pool_fold_kernel,
        out_shape=jax.ShapeDtypeStruct(w_pool_up.shape, BF16),
        compiler_params=pltpu.CompilerParams(vmem_limit_bytes=VMEM_LIMIT_BYTES),
        name="pool_fold",
    )(w_grp, pool_scale, w_pool_up)


def _rope_fold_kernel(cos_ref, sin_ref, gq_ref, gk_ref, cq_ref, sq_ref, ck_ref, sk_ref):
    cos = cos_ref[...]
    sin = sin_ref[...]
    for g_ref, c_ref, s_ref, logit_scale in ((gq_ref, cq_ref, sq_ref, Q_SCALE), (gk_ref, ck_ref, sk_ref, 1.0)):
        g = jnp.broadcast_to(g_ref[...], (F32_SUBLANES, HEAD_DIM)) * (logit_scale * math.sqrt(HEAD_DIM))
        g_partner = pltpu.roll(g, ROPE_PARTNER, 1)
        c_ref[...] = cos * g[0:1, :]
        s_ref[...] = sin * g_partner[0:1, :]


def _rope_fold(cos_t, sin_t, g_q, g_k):
    tab = jax.ShapeDtypeStruct(cos_t.shape, F32)
    return pl.pallas_call(
        _rope_fold_kernel,
        out_shape=[tab] * 4,
        compiler_params=pltpu.CompilerParams(vmem_limit_bytes=VMEM_LIMIT_BYTES),
        name="rope_fold",
    )(cos_t, sin_t, g_q, g_k)


def _in_proj_kernel(x_ref, xp_ref, xn_ref, mod_ref, gpre_ref, w_ref, bg_ref,
                    cq_ref, sq_ref, ck_ref, sk_ref, q_ref, k_ref, v_ref, pool_ref, gate_ref,
                    *, tm, sub, n_tiles, seq):
    i = pl.program_id(1)
    gpre = gpre_ref[...]
    shift = mod_ref[0, 0:1, :]
    scale = mod_ref[0, 1:2, :]
    subs = [slice(s * sub, (s + 1) * sub) for s in range(tm // sub)]
    hb = [_norm_mod(x_ref[0, rs, :], gpre, shift, scale).astype(BF16) for rs in subs]

    w_pool = w_ref[:, POOL_OFF:POOL_OFF + POOL_WIDTH]
    heads_per_dot = MXU_WIDTH // HEAD_DIM
    gate_chunk = GATE_WIDTH // (ATTN_WIDTH // MXU_WIDTH)

    def norm_rope(t, cos_g, sin_g):
        inv = lax.rsqrt(jnp.sum(t * t, axis=-1, keepdims=True) + HEAD_DIM * EPS)
        return ((t * cos_g + pltpu.roll(t, ROPE_PARTNER, 1) * sin_g) * inv).astype(BF16)

    def pool_windows(u_sub):
        u_main = jnp.concatenate(u_sub, axis=0)
        x_halo = jnp.concatenate([xp_ref[0], xn_ref[0]], axis=0)
        u_halo = _dot(_norm_mod(x_halo, gpre, shift, scale).astype(BF16), w_pool)
        u_prev = jnp.where(i > 0, u_halo[:POOL_HALO], 0.0)
        u_next = jnp.where(i < n_tiles - 1, u_halo[POOL_HALO:], 0.0)
        u_ext = jnp.concatenate([u_prev, u_main, u_next], axis=0)
        n_ext = tm + 2 * POOL_HALO
        edge_iota = lax.broadcasted_iota(jnp.int32, (POOL_HALO, POOL_GROUP_WIDTH), 0)
        edge_t = (i * tm + edge_iota, i * tm + (tm - POOL_HALO) + edge_iota)
        for gi, win in enumerate(POOL_WINDOWS):
            sl = slice(gi * POOL_GROUP_WIDTH, (gi + 1) * POOL_GROUP_WIDTH)
            half = win // 2
            run = u_ext[:, sl]
            length = 1
            while length < half:
                run = run + pltpu.roll(run, n_ext - length, 0)
                length *= 2
            window = run + pltpu.roll(run, half, 0)
            window = window[POOL_HALO:POOL_HALO + tm]
            inv_cnt = [1.0 / (jnp.minimum(t + half, seq) - jnp.maximum(t - half, 0)).astype(F32)
                       for t in edge_t]
            inv_cnt = jnp.concatenate(
                [inv_cnt[0], jnp.full((tm - 2 * POOL_HALO, POOL_GROUP_WIDTH), 1.0 / win, F32), inv_cnt[1]],
                axis=0)
            pool_ref[0, :, sl] = (window * inv_cnt - u_main[:, sl]).astype(BF16)

    u_sub = []
    for si, (rs, h) in enumerate(zip(subs, hb)):
        u_sub.append(_dot(h, w_pool))
        ts = pl.ds(pl.multiple_of(i * tm + si * sub, sub), sub)
        cq, sq = cq_ref[ts, :], sq_ref[ts, :]
        for j in range(ATTN_WIDTH // MXU_WIDTH):
            pq = _dot(h, w_ref[:, j * MXU_WIDTH:(j + 1) * MXU_WIDTH])
            for e in range(heads_per_dot):
                t = pq[:, e * HEAD_DIM:(e + 1) * HEAD_DIM]
                q_ref[0, j * heads_per_dot + e, rs, :] = norm_rope(t, cq, sq)
        pk = _dot(h, w_ref[:, K_OFF:K_OFF + KV_WIDTH])
        pv = _dot(h, w_ref[:, V_OFF:V_OFF + KV_WIDTH])
        for e in range(N_KV_HEADS):
            sl = slice(e * HEAD_DIM, (e + 1) * HEAD_DIM)
            k_ref[0, e, rs, :] = norm_rope(pk[:, sl], ck_ref[ts, :], sk_ref[ts, :])
            v_ref[0, e, rs, :] = pv[:, sl].astype(BF16)
    pool_windows(u_sub)
    half_bias = 0.5 * bg_ref[...]
    for rs, h in zip(subs, hb):
        for j in range(GATE_WIDTH // gate_chunk):
            sl = slice(j * gate_chunk, (j + 1) * gate_chunk)
            half_logits = _dot(h, w_ref[:, GATE_OFF + j * gate_chunk:GATE_OFF + (j + 1) * gate_chunk])
            gate_ref[0, rs, sl] = (0.5 * jnp.tanh(half_logits + half_bias[:, sl]) + 0.5).astype(BF16)


def _in_proj(x, mod, g_pre, w_in, b_gate, rope_tabs, *, tm, sub):
    b, s, _ = x.shape
    n_tiles = s // tm
    halo_blocks_per_tile = tm // POOL_HALO
    last_halo_block = s // POOL_HALO - 1
    const2 = lambda bi, i: (0, 0)
    in_specs = [
        pl.BlockSpec((1, tm, D_MODEL), lambda bi, i: (bi, i, 0)),
        pl.BlockSpec((1, POOL_HALO, D_MODEL),
                     lambda bi, i: (bi, jnp.maximum(i * halo_blocks_per_tile - 1, 0), 0)),
        pl.BlockSpec((1, POOL_HALO, D_MODEL),
                     lambda bi, i: (bi, jnp.minimum((i + 1) * halo_blocks_per_tile, last_halo_block), 0)),
        pl.BlockSpec((1, N_MOD, D_MODEL), lambda bi, i: (bi, 0, 0)),
        pl.BlockSpec((1, D_MODEL), const2),
        _VMEM_RESIDENT,
        pl.BlockSpec((1, GATE_WIDTH), const2),
    ] + [_VMEM_RESIDENT] * len(rope_tabs)
    out_specs = [
        pl.BlockSpec((1, N_Q_HEADS, tm, HEAD_DIM), lambda bi, i: (bi, 0, i, 0)),
        pl.BlockSpec((1, N_KV_HEADS, tm, HEAD_DIM), lambda bi, i: (bi, 0, i, 0)),
        pl.BlockSpec((1, N_KV_HEADS, tm, HEAD_DIM), lambda bi, i: (bi, 0, i, 0)),
        pl.BlockSpec((1, tm, POOL_WIDTH), lambda bi, i: (bi, i, 0)),
        pl.BlockSpec((1, tm, GATE_WIDTH), lambda bi, i: (bi, i, 0)),
    ]
    out_shape = [
        jax.ShapeDtypeStruct((b, N_Q_HEADS, s, HEAD_DIM), BF16),
        jax.ShapeDtypeStruct((b, N_KV_HEADS, s, HEAD_DIM), BF16),
        jax.ShapeDtypeStruct((b, N_KV_HEADS, s, HEAD_DIM), BF16),
        jax.ShapeDtypeStruct((b, s, POOL_WIDTH), BF16),
        jax.ShapeDtypeStruct((b, s, GATE_WIDTH), BF16),
    ]
    return pl.pallas_call(
        functools.partial(_in_proj_kernel, tm=tm, sub=sub, n_tiles=n_tiles, seq=s),
        grid=(b, n_tiles),
        in_specs=in_specs,
        out_specs=out_specs,
        out_shape=out_shape,
        compiler_params=_params("parallel", "parallel"),
        name="in_proj",
    )(x, x, x, mod, g_pre, w_in, b_gate, *rope_tabs)


def _attention_kernel(q_ref, kc_ref, vc_ref, kl_ref, vl_ref, o_ref, *, tq, rows):
    kc = kc_ref[0, 0]
    kl = kl_ref[0, 0]
    ones_cols = lambda n: jnp.ones((n, MXU_WIDTH - HEAD_DIM), BF16)
    vc = jnp.concatenate([vc_ref[0, 0], ones_cols(kc.shape[0])], axis=-1)
    vl = jnp.concatenate([vl_ref[0, 0], ones_cols(kl.shape[0])], axis=-1)
    for h in range(Q_GROUP):
        for r in range(tq // rows):
            rs = slice(r * rows, (r + 1) * rows)
            q = q_ref[0, 0, h, rs, :]
            s_c = _dot_nt(q, kc)
            s_l = _dot_nt(q, kl)
            m = jnp.maximum(jnp.max(s_c, axis=-1, keepdims=True),
                            jnp.max(s_l, axis=-1, keepdims=True))
            p_c = jnp.exp2(s_c - m).astype(BF16)
            p_l = jnp.exp2(s_l - m).astype(BF16)
            o = _dot(p_c, vc) + _dot(p_l, vl)
            o_ref[0, rs, h * HEAD_DIM:(h + 1) * HEAD_DIM] = (
                o[:, :HEAD_DIM] / o[:, HEAD_DIM:]).astype(BF16)


def _attention(q, k_ctx, v_ctx, k_lat, v_lat, *, tq, rows):
    b, _, s, _ = q.shape
    n_ctx = k_ctx.shape[2]
    qg = q.reshape(b, N_KV_HEADS, Q_GROUP, s, HEAD_DIM)
    kv_idx = lambda bi, kh, i: (bi, kh, 0, 0)
    return pl.pallas_call(
        functools.partial(_attention_kernel, tq=tq, rows=rows),
        grid=(b, N_KV_HEADS, s // tq),
        in_specs=[pl.BlockSpec((1, 1, Q_GROUP, tq, HEAD_DIM), lambda bi, kh, i: (bi, kh, 0, i, 0)),
                  pl.BlockSpec((1, 1, n_ctx, HEAD_DIM), kv_idx),
                  pl.BlockSpec((1, 1, n_ctx, HEAD_DIM), kv_idx),
                  pl.BlockSpec((1, 1, s, HEAD_DIM), kv_idx),
                  pl.BlockSpec((1, 1, s, HEAD_DIM), kv_idx)],
        out_specs=pl.BlockSpec((1, tq, Q_GROUP * HEAD_DIM), lambda bi, kh, i: (bi, i, kh)),
        out_shape=jax.ShapeDtypeStruct((b, s, ATTN_WIDTH), BF16),
        compiler_params=_params("parallel", "parallel", "parallel"),
        name="attention",
    )(qg, k_ctx, v_ctx, k_lat, v_lat)


def _merge_mlp_kernel(x_ref, a_ref, p_ref, g_ref, mod_ref, gmix_ref, gpre_ref, gpost_ref,
                      wa_ref, wp_ref, wo_ref, w1_ref, w2_ref, o_ref, *, tm, sub, ff_chunk):
    subs = [slice(s * sub, (s + 1) * sub) for s in range(tm // sub)]
    x_mid = []
    for rs in subs:
        ya = _dot(a_ref[0, rs, :], wa_ref[...]).astype(BF16)
        yp = _dot(p_ref[0, rs, :], wp_ref[...]).astype(BF16)
        z = g_ref[0, rs, :D_MODEL] * ya + g_ref[0, rs, D_MODEL:] * yp
        y = _dot(z, wo_ref[...])
        x_mid.append(x_ref[0, rs, :] + _rms(y, gmix_ref[...] * mod_ref[0, 2:3, :]))
    for rs, xs in zip(subs, x_mid):
        h = _norm_mod(xs, gpre_ref[...], mod_ref[0, 3:4, :], mod_ref[0, 4:5, :]).astype(BF16)
        acc = None
        for j in range(D_FF // ff_chunk):
            sl = slice(j * ff_chunk, (j + 1) * ff_chunk)
            t = jnp.maximum(_dot(h, w1_ref[:, sl]), 0.0)
            part = _dot((t * t).astype(BF16), w2_ref[sl, :])
            acc = part if acc is None else acc + part
        o_ref[0, rs, :] = xs + _rms(acc, gpost_ref[...] * mod_ref[0, 5:6, :])


def _merge_mlp(x, attn_o, pooled, gates, mod, g_post_mix, g_pre_mlp, g_post_mlp,
               w_attn_up, w_pool_fold, w_out, w_ff1, w_ff2, *, tm, sub, ff_chunk):
    b, s, _ = x.shape
    tile = lambda width: pl.BlockSpec((1, tm, width), lambda bi, i: (bi, i, 0))
    const2 = lambda bi, i: (0, 0)
    gain = pl.BlockSpec((1, D_MODEL), const2)
    return pl.pallas_call(
        functools.partial(_merge_mlp_kernel, tm=tm, sub=sub, ff_chunk=ff_chunk),
        grid=(b, s // tm),
        in_specs=[tile(D_MODEL), tile(ATTN_WIDTH), tile(POOL_WIDTH), tile(GATE_WIDTH),
                  pl.BlockSpec((1, N_MOD, D_MODEL), lambda bi, i: (bi, 0, 0)),
                  gain, gain, gain] + [_VMEM_RESIDENT] * 5,
        out_specs=tile(D_MODEL),
        out_shape=jax.ShapeDtypeStruct(x.shape, F32),
        compiler_params=_params("parallel", "parallel"),
        name="merge_mlp",
    )(x, attn_o, pooled, gates, mod, g_post_mix, g_pre_mlp, g_post_mlp,
      w_attn_up, w_pool_fold, w_out, w_ff1, w_ff2)


def _rope_tables(seq):
    t = np.arange(seq)
    rows = (t // GRID_W).astype(np.float64)
    cols = (t % GRID_W).astype(np.float64)
    freqs = ROPE_THETA ** (-np.arange(ROPE_FREQS, dtype=np.float64) / ROPE_FREQS)
    ang_row = rows[:, None] * freqs
    ang_col = cols[:, None] * freqs
    cos_r, sin_r, cos_c, sin_c = np.cos(ang_row), np.sin(ang_row), np.cos(ang_col), np.sin(ang_col)
    cos_t = np.concatenate([cos_r, cos_c, cos_r, cos_c], axis=-1).astype(np.float32)
    sin_t = np.concatenate([-sin_r, -sin_c, sin_r, sin_c], axis=-1).astype(np.float32)
    return jnp.asarray(cos_t), jnp.asarray(sin_t)


def kernel(x, c, ctx, c_ctx, w_mod, b_mod, g_pre_mix, g_post_mix, g_pre_mlp, g_post_mlp, w_in, b_gate, g_q, g_k, w_attn_up, w_pool_grp, pool_scale, w_pool_up, w_out, w_ff1, w_ff2):
    depth = w_mod.shape[0]
    assert depth == 1, "single-layer block"
    b, s, _ = x.shape
    row = lambda a: a[0].reshape(1, -1)

    n_cond = -(-(b + 1) // F32_SUBLANES) * F32_SUBLANES
    cond = jnp.concatenate([c, c_ctx[None, :], jnp.zeros((n_cond - b - 1, D_MODEL), F32)], axis=0)
    mod_all = _adaln(cond, w_mod[0], b_mod[0])
    mod = mod_all[:b].reshape(b, N_MOD, D_MODEL)
    mod_ctx = mod_all[b:b + 1].reshape(1, N_MOD, D_MODEL)

    w_in_b = _w_in_prep(w_in[0])
    g_q_p = _permute_head_dim(row(g_q))
    g_k_p = _permute_head_dim(row(g_k))
    cos_t, sin_t = _rope_tables(s)

    k_ctx, v_ctx = _ctx_kv(ctx, mod_ctx, row(g_pre_mix), w_in_b[:, K_OFF:POOL_OFF], g_k_p, samples=8)
    rope_tabs = _rope_fold(cos_t, sin_t, g_q_p, g_k_p)
    q, k_lat, v_lat, pooled, gates = _in_proj(
        x, mod, row(g_pre_mix), w_in_b, row(b_gate), rope_tabs, tm=1024, sub=256)
    attn_o = _attention(q, k_ctx, v_ctx, k_lat, v_lat, tq=2048, rows=128)
    w_pool_fold = _pool_fold(w_pool_grp[0], row(pool_scale), w_pool_up[0])
    return _merge_mlp(x, attn_o, pooled, gates, mod, row(g_post_mix), row(g_pre_mlp), row(g_post_mlp),
                      w_attn_up[0].astype(BF16), w_pool_fold, w_out[0].astype(BF16),
                      w_ff1[0].astype(BF16), w_ff2[0].astype(BF16), tm=1024, sub=256, ff_chunk=1024)
```

```python
import functools
import math

import jax
import jax.numpy as jnp
import numpy as np
from jax import lax
from jax.experimental import pallas as pl
from jax.experimental.pallas import tpu as pltpu

D_MODEL = 1024
GRID_W = 64
HEAD_DIM = 128
N_Q_HEADS = 8
N_KV_HEADS = 2
Q_GROUP = N_Q_HEADS // N_KV_HEADS
ATTN_WIDTH = N_Q_HEADS * HEAD_DIM
KV_WIDTH = N_KV_HEADS * HEAD_DIM
POOL_WINDOWS = (2, 4, 8, 16)
POOL_WIDTH = 512
POOL_GROUP_WIDTH = POOL_WIDTH // len(POOL_WINDOWS)
GATE_WIDTH = 2 * D_MODEL
K_OFF = ATTN_WIDTH
V_OFF = K_OFF + KV_WIDTH
POOL_OFF = V_OFF + KV_WIDTH
GATE_OFF = POOL_OFF + POOL_WIDTH
D_FF = 4 * D_MODEL
ROPE_THETA = 10000.0
ROPE_FREQS = HEAD_DIM // 4
ROPE_PARTNER = HEAD_DIM // 2
EPS = 1e-6
N_MOD = 6

F32_SUBLANES = 8
MXU_WIDTH = 256
VMEM_LIMIT_BYTES = 62 * 1024 * 1024

POOL_HALO = max(POOL_WINDOWS) // 2
assert POOL_HALO == F32_SUBLANES

Q_SCALE = (HEAD_DIM ** -0.5) * math.log2(math.e)

BF16 = jnp.bfloat16
F32 = jnp.float32


def _dot(a, b):
    return jnp.dot(a, b, preferred_element_type=F32)


def _dot_nt(a, b):
    return lax.dot_general(a, b, (((1,), (1,)), ((), ())), preferred_element_type=F32)


def _rms(x, gain):
    ms = jnp.mean(x * x, axis=-1, keepdims=True)
    return x * lax.rsqrt(ms + EPS) * gain


def _norm_mod(x, gain, shift, scale):
    return _rms(x, gain * (1.0 + scale)) + shift


def _permute_head_dim(a):
    lead = a.shape[:-1]
    blocks = a.reshape(*lead, -1, 2, 2, ROPE_FREQS)
    return jnp.swapaxes(blocks, -3, -2).reshape(a.shape)


_VMEM_RESIDENT = pl.BlockSpec(memory_space=pltpu.VMEM)


def _params(*sem):
    return pltpu.CompilerParams(dimension_semantics=sem, vmem_limit_bytes=VMEM_LIMIT_BYTES)


def _w_in_prep_kernel(w_ref, o_ref, *, bn):
    is_qk = pl.program_id(0) * bn < V_OFF

    @pl.when(is_qk)
    def _():
        block = lax.broadcasted_iota(jnp.int32, (w_ref.shape[0], HEAD_DIM), 1) // ROPE_FREQS
        for e in range(bn // HEAD_DIM):
            sl = slice(e * HEAD_DIM, (e + 1) * HEAD_DIM)
            t = w_ref[:, sl]
            swapped = jnp.where(block == 1, pltpu.roll(t, HEAD_DIM - ROPE_FREQS, 1),
                                jnp.where(block == 2, pltpu.roll(t, ROPE_FREQS, 1), t))
            o_ref[:, sl] = swapped.astype(BF16)

    @pl.when(jnp.logical_not(is_qk))
    def _():
        o_ref[...] = w_ref[...].astype(BF16)


def _w_in_prep(w_in):
    k, n = w_in.shape
    bn = MXU_WIDTH
    assert V_OFF % bn == 0
    return pl.pallas_call(
        functools.partial(_w_in_prep_kernel, bn=bn),
        grid=(n // bn,),
        in_specs=[pl.BlockSpec((k, bn), lambda j: (0, j))],
        out_specs=pl.BlockSpec((k, bn), lambda j: (0, j)),
        out_shape=jax.ShapeDtypeStruct((k, n), BF16),
        compiler_params=_params("parallel"),
        name="w_in_prep",
    )(w_in)


def _adaln_kernel(c_ref, w_ref, b_ref, o_ref):
    c = c_ref[...]
    a = (c * jax.nn.sigmoid(c)).astype(BF16)
    o_ref[...] = _dot(a, w_ref[...].astype(BF16)) + b_ref[...]


def _adaln(cond, w_mod, b_mod):
    rows = cond.shape[0]
    n_out = w_mod.shape[1]
    bn = D_MODEL
    return pl.pallas_call(
        _adaln_kernel,
        grid=(n_out // bn,),
        in_specs=[pl.BlockSpec((rows, D_MODEL), lambda j: (0, 0)),
                  pl.BlockSpec((D_MODEL, bn), lambda j: (0, j)),
                  pl.BlockSpec((1, bn), lambda j: (0, j))],
        out_specs=pl.BlockSpec((rows, bn), lambda j: (0, j)),
        out_shape=jax.ShapeDtypeStruct((rows, n_out), F32),
        compiler_params=_params("parallel"),
        name="adaln",
    )(cond, w_mod, b_mod.reshape(1, n_out))


def _ctx_kv_kernel(x_ref, mod_ref, gpre_ref, w_ref, gk_ref, k_ref, v_ref):
    for s in range(x_ref.shape[0]):
        h = _norm_mod(x_ref[s], gpre_ref[...], mod_ref[0, 0:1, :], mod_ref[0, 1:2, :]).astype(BF16)
        pk = _dot(h, w_ref[:, :KV_WIDTH])
        pv = _dot(h, w_ref[:, KV_WIDTH:])
        for e in range(N_KV_HEADS):
            sl = slice(e * HEAD_DIM, (e + 1) * HEAD_DIM)
            k_ref[s, e] = _rms(pk[:, sl], gk_ref[...]).astype(BF16)
            v_ref[s, e] = pv[:, sl].astype(BF16)


def _ctx_kv(ctx, mod_ctx, g_pre, w_kv, g_k, *, samples):
    b, n, _ = ctx.shape
    kv_shape = jax.ShapeDtypeStruct((b, N_KV_HEADS, n, HEAD_DIM), BF16)
    kv_spec = pl.BlockSpec((samples, N_KV_HEADS, n, HEAD_DIM), lambda i: (i, 0, 0, 0))
    return pl.pallas_call(
        _ctx_kv_kernel,
        grid=(b // samples,),
        in_specs=[pl.BlockSpec((samples, n, D_MODEL), lambda i: (i, 0, 0)),
                  pl.BlockSpec((1, N_MOD, D_MODEL), lambda i: (0, 0, 0)),
                  pl.BlockSpec((1, D_MODEL), lambda i: (0, 0)),
                  pl.BlockSpec((D_MODEL, 2 * KV_WIDTH), lambda i: (0, 0)),
                  pl.BlockSpec((1, HEAD_DIM), lambda i: (0, 0))],
        out_specs=[kv_spec, kv_spec],
        out_shape=[kv_shape, kv_shape],
        compiler_params=_params("parallel"),
        name="ctx_kv",
    )(ctx, mod_ctx, g_pre, w_kv, g_k)


def _pool_fold_kernel(wg_ref, ps_ref, wu_ref, o_ref):
    for g in range(len(POOL_WINDOWS)):
        sl = slice(g * POOL_GROUP_WIDTH, (g + 1) * POOL_GROUP_WIDTH)
        o_ref[sl, :] = jnp.dot(wg_ref[g] * ps_ref[:, sl], wu_ref[sl, :], precision=lax.Precision.HIGHEST,
                               preferred_element_type=F32).astype(BF16)


def _pool_fold(w_grp, pool_scale, w_pool_up):
    return pl.pallas_call(
        _pool_fold_kernel,
        out_shape=jax.ShapeDtypeStruct(w_pool_up.shape, BF16),
        compiler_params=pltpu.CompilerParams(vmem_limit_bytes=VMEM_LIMIT_BYTES),
        name="pool_fold",
    )(w_grp, pool_scale, w_pool_up)


def _rope_fold_kernel(cos_ref, sin_ref, gq_ref, gk_ref, cq_ref, sq_ref, ck_ref, sk_ref):
    cos = cos_ref[...]
    sin = sin_ref[...]
    for g_ref, c_ref, s_ref, logit_scale in ((gq_ref, cq_ref, sq_ref, Q_SCALE), (gk_ref, ck_ref, sk_ref, 1.0)):
        g = jnp.broadcast_to(g_ref[...], (F32_SUBLANES, HEAD_DIM)) * (logit_scale * math.sqrt(HEAD_DIM))
        g_partner = pltpu.roll(g, ROPE_PARTNER, 1)
        c_ref[...] = cos * g[0:1, :]
        s_ref[...] = sin * g_partner[0:1, :]


def _rope_fold(cos_t, sin_t, g_q, g_k):
    tab = jax.ShapeDtypeStruct(cos_t.shape, F32)
    return pl.pallas_call(
        _rope_fold_kernel,
        out_shape=[tab] * 4,
        compiler_params=pltpu.CompilerParams(vmem_limit_bytes=VMEM_LIMIT_BYTES),
        name="rope_fold",
    )(cos_t, sin_t, g_q, g_k)


def _in_proj_kernel(x_ref, xp_ref, xn_ref, mod_ref, gpre_ref, w_ref, bg_ref,
                    cq_ref, sq_ref, ck_ref, sk_ref, q_ref, k_ref, v_ref, pool_ref, gate_ref,
                    *, tm, sub, n_tiles, seq):
    i = pl.program_id(1)
    gpre = gpre_ref[...]
    shift = mod_ref[0, 0:1, :]
    scale = mod_ref[0, 1:2, :]
    subs = [slice(s * sub, (s + 1) * sub) for s in range(tm // sub)]
    hb = [_norm_mod(x_ref[0, rs, :], gpre, shift, scale).astype(BF16) for rs in subs]

    w_pool = w_ref[:, POOL_OFF:POOL_OFF + POOL_WIDTH]
    heads_per_dot = MXU_WIDTH // HEAD_DIM
    gate_chunk = GATE_WIDTH // (ATTN_WIDTH // MXU_WIDTH)

    def norm_rope(t, cos_g, sin_g):
        inv = lax.rsqrt(jnp.sum(t * t, axis=-1, keepdims=True) + HEAD_DIM * EPS)
        return ((t * cos_g + pltpu.roll(t, ROPE_PARTNER, 1) * sin_g) * inv).astype(BF16)

    def pool_windows(u_sub):
        u_main = jnp.concatenate(u_sub, axis=0)
        x_halo = jnp.concatenate([xp_ref[0], xn_ref[0]], axis=0)
        u_halo = _dot(_norm_mod(x_halo, gpre, shift, scale).astype(BF16), w_pool)
        u_prev = jnp.where(i > 0, u_halo[:POOL_HALO], 0.0)
        u_next = jnp.where(i < n_tiles - 1, u_halo[POOL_HALO:], 0.0)
        u_ext = jnp.concatenate([u_prev, u_main, u_next], axis=0)
        n_ext = tm + 2 * POOL_HALO
        edge_iota = lax.broadcasted_iota(jnp.int32, (POOL_HALO, POOL_GROUP_WIDTH), 0)
        edge_t = (i * tm + edge_iota, i * tm + (tm - POOL_HALO) + edge_iota)
        for gi, win in enumerate(POOL_WINDOWS):
            sl = slice(gi * POOL_GROUP_WIDTH, (gi + 1) * POOL_GROUP_WIDTH)
            half = win // 2
            run = u_ext[:, sl]
            length = 1
            while length < half:
                run = run + pltpu.roll(run, n_ext - length, 0)
                length *= 2
            window = run + pltpu.roll(run, half, 0)
            window = window[POOL_HALO:POOL_HALO + tm]
            inv_cnt = [1.0 / (jnp.minimum(t + half, seq) - jnp.maximum(t - half, 0)).astype(F32)
                       for t in edge_t]
            inv_cnt = jnp.concatenate(
                [inv_cnt[0], jnp.full((tm - 2 * POOL_HALO, POOL_GROUP_WIDTH), 1.0 / win, F32), inv_cnt[1]],
                axis=0)
            pool_ref[0, :, sl] = (window * inv_cnt - u_main[:, sl]).astype(BF16)

    u_sub = []
    for si, (rs, h) in enumerate(zip(subs, hb)):
        u_sub.append(_dot(h, w_pool))
        ts = pl.ds(pl.multiple_of(i * tm + si * sub, sub), sub)
        cq, sq = cq_ref[ts, :], sq_ref[ts, :]
        for j in range(ATTN_WIDTH // MXU_WIDTH):
            pq = _dot(h, w_ref[:, j * MXU_WIDTH:(j + 1) * MXU_WIDTH])
            for e in range(heads_per_dot):
                t = pq[:, e * HEAD_DIM:(e + 1) * HEAD_DIM]
                q_ref[0, j * heads_per_dot + e, rs, :] = norm_rope(t, cq, sq)
        pk = _dot(h, w_ref[:, K_OFF:K_OFF + KV_WIDTH])
        pv = _dot(h, w_ref[:, V_OFF:V_OFF + KV_WIDTH])
        for e in range(N_KV_HEADS):
            sl = slice(e * HEAD_DIM, (e + 1) * HEAD_DIM)
            k_ref[0, e, rs, :] = norm_rope(pk[:, sl], ck_ref[ts, :], sk_ref[ts, :])
            v_ref[0, e, rs, :] = pv[:, sl].astype(BF16)
    pool_windows(u_sub)
    for rs, h in zip(subs, hb):
        for j in range(GATE_WIDTH // gate_chunk):
            sl = slice(j * gate_chunk, (j + 1) * gate_chunk)
            logits = _dot(h, w_ref[:, GATE_OFF + j * gate_chunk:GATE_OFF + (j + 1) * gate_chunk])
            gate_ref[0, rs, sl] = jax.nn.sigmoid(logits + bg_ref[:, sl]).astype(BF16)


def _in_proj(x, mod, g_pre, w_in, b_gate, rope_tabs, *, tm, sub):
    b, s, _ = x.shape
    n_tiles = s // tm
    halo_blocks_per_tile = tm // POOL_HALO
    last_halo_block = s // POOL_HALO - 1
    const2 = lambda bi, i: (0, 0)
    in_specs = [
        pl.BlockSpec((1, tm, D_MODEL), lambda bi, i: (bi, i, 0)),
        pl.BlockSpec((1, POOL_HALO, D_MODEL),
                     lambda bi, i: (bi, jnp.maximum(i * halo_blocks_per_tile - 1, 0), 0)),
        pl.BlockSpec((1, POOL_HALO, D_MODEL),
                     lambda bi, i: (bi, jnp.minimum((i + 1) * halo_blocks_per_tile, last_halo_block), 0)),
        pl.BlockSpec((1, N_MOD, D_MODEL), lambda bi, i: (bi, 0, 0)),
        pl.BlockSpec((1, D_MODEL), const2),
        _VMEM_RESIDENT,
        pl.BlockSpec((1, GATE_WIDTH), const2),
    ] + [_VMEM_RESIDENT] * len(rope_tabs)
    out_specs = [
        pl.BlockSpec((1, N_Q_HEADS, tm, HEAD_DIM), lambda bi, i: (bi, 0, i, 0)),
        pl.BlockSpec((1, N_KV_HEADS, tm, HEAD_DIM), lambda bi, i: (bi, 0, i, 0)),
        pl.BlockSpec((1, N_KV_HEADS, tm, HEAD_DIM), lambda bi, i: (bi, 0, i, 0)),
        pl.BlockSpec((1, tm, POOL_WIDTH), lambda bi, i: (bi, i, 0)),
        pl.BlockSpec((1, tm, GATE_WIDTH), lambda bi, i: (bi, i, 0)),
    ]
    out_shape = [
        jax.ShapeDtypeStruct((b, N_Q_HEADS, s, HEAD_DIM), BF16),
        jax.ShapeDtypeStruct((b, N_KV_HEADS, s, HEAD_DIM), BF16),
        jax.ShapeDtypeStruct((b, N_KV_HEADS, s, HEAD_DIM), BF16),
        jax.ShapeDtypeStruct((b, s, POOL_WIDTH), BF16),
        jax.ShapeDtypeStruct((b, s, GATE_WIDTH), BF16),
    ]
    return pl.pallas_call(
        functools.partial(_in_proj_kernel, tm=tm, sub=sub, n_tiles=n_tiles, seq=s),
        grid=(b, n_tiles),
        in_specs=in_specs,
        out_specs=out_specs,
        out_shape=out_shape,
        compiler_params=_params("parallel", "parallel"),
        name="in_proj",
    )(x, x, x, mod, g_pre, w_in, b_gate, *rope_tabs)


def _attention_kernel(q_ref, kc_ref, vc_ref, kl_ref, vl_ref, o_ref, *, tq, rows):
    kc = kc_ref[0, 0]
    kl = kl_ref[0, 0]
    ones_cols = lambda n: jnp.ones((n, MXU_WIDTH - HEAD_DIM), BF16)
    vc = jnp.concatenate([vc_ref[0, 0], ones_cols(kc.shape[0])], axis=-1)
    vl = jnp.concatenate([vl_ref[0, 0], ones_cols(kl.shape[0])], axis=-1)
    for h in range(Q_GROUP):
        for r in range(tq // rows):
            rs = slice(r * rows, (r + 1) * rows)
            q = q_ref[0, 0, h, rs, :]
            s_c = _dot_nt(q, kc)
            s_l = _dot_nt(q, kl)
            m = jnp.maximum(jnp.max(s_c, axis=-1, keepdims=True),
                            jnp.max(s_l, axis=-1, keepdims=True))
            p_c = jnp.exp2(s_c - m).astype(BF16)
            p_l = jnp.exp2(s_l - m).astype(BF16)
            o = _dot(p_c, vc) + _dot(p_l, vl)
            o_ref[0, rs, h * HEAD_DIM:(h + 1) * HEAD_DIM] = (
                o[:, :HEAD_DIM] / o[:, HEAD_DIM:]).astype(BF16)


def _attention(q, k_ctx, v_ctx, k_lat, v_lat, *, tq, rows):
    b, _, s, _ = q.shape
    n_ctx = k_ctx.shape[2]
    qg = q.reshape(b, N_KV_HEADS, Q_GROUP, s, HEAD_DIM)
    kv_idx = lambda bi, kh, i: (bi, kh, 0, 0)
    return pl.pallas_call(
        functools.partial(_attention_kernel, tq=tq, rows=rows),
        grid=(b, N_KV_HEADS, s // tq),
        in_specs=[pl.BlockSpec((1, 1, Q_GROUP, tq, HEAD_DIM), lambda bi, kh, i: (bi, kh, 0, i, 0)),
                  pl.BlockSpec((1, 1, n_ctx, HEAD_DIM), kv_idx),
                  pl.BlockSpec((1, 1, n_ctx, HEAD_DIM), kv_idx),
                  pl.BlockSpec((1, 1, s, HEAD_DIM), kv_idx),
                  pl.BlockSpec((1, 1, s, HEAD_DIM), kv_idx)],
        out_specs=pl.BlockSpec((1, tq, Q_GROUP * HEAD_DIM), lambda bi, kh, i: (bi, i, kh)),
        out_shape=jax.ShapeDtypeStruct((b, s, ATTN_WIDTH), BF16),
        compiler_params=_params("parallel", "parallel", "parallel"),
        name="attention",
    )(qg, k_ctx, v_ctx, k_lat, v_lat)


def _merge_mlp_kernel(x_ref, a_ref, p_ref, g_ref, mod_ref, gmix_ref, gpre_ref, gpost_ref,
                      wa_ref, wp_ref, wo_ref, w1_ref, w2_ref, o_ref, *, tm, sub, ff_chunk):
    subs = [slice(s * sub, (s + 1) * sub) for s in range(tm // sub)]
    x_mid = []
    for rs in subs:
        ya = _dot(a_ref[0, rs, :], wa_ref[...]).astype(BF16)
        yp = _dot(p_ref[0, rs, :], wp_ref[...]).astype(BF16)
        z = g_ref[0, rs, :D_MODEL] * ya + g_ref[0, rs, D_MODEL:] * yp
        y = _dot(z, wo_ref[...])
        x_mid.append(x_ref[0, rs, :] + _rms(y, gmix_ref[...] * mod_ref[0, 2:3, :]))
    for rs, xs in zip(subs, x_mid):
        h = _norm_mod(xs, gpre_ref[...], mod_ref[0, 3:4, :], mod_ref[0, 4:5, :]).astype(BF16)
        acc = None
        for j in range(D_FF // ff_chunk):
            sl = slice(j * ff_chunk, (j + 1) * ff_chunk)
            t = jnp.maximum(_dot(h, w1_ref[:, sl]).astype(BF16), 0)
            part = _dot(t * t, w2_ref[sl, :])
            acc = part if acc is None else acc + part
        o_ref[0, rs, :] = xs + _rms(acc, gpost_ref[...] * mod_ref[0, 5:6, :])


def _merge_mlp(x, attn_o, pooled, gates, mod, g_post_mix, g_pre_mlp, g_post_mlp,
               w_attn_up, w_pool_fold, w_out, w_ff1, w_ff2, *, tm, sub, ff_chunk):
    b, s, _ = x.shape
    tile = lambda width: pl.BlockSpec((1, tm, width), lambda bi, i: (bi, i, 0))
    const2 = lambda bi, i: (0, 0)
    gain = pl.BlockSpec((1, D_MODEL), const2)
    return pl.pallas_call(
        functools.partial(_merge_mlp_kernel, tm=tm, sub=sub, ff_chunk=ff_chunk),
        grid=(b, s // tm),
        in_specs=[tile(D_MODEL), tile(ATTN_WIDTH), tile(POOL_WIDTH), tile(GATE_WIDTH),
                  pl.BlockSpec((1, N_MOD, D_MODEL), lambda bi, i: (bi, 0, 0)),
                  gain, gain, gain] + [_VMEM_RESIDENT] * 5,
        out_specs=tile(D_MODEL),
        out_shape=jax.ShapeDtypeStruct(x.shape, F32),
        compiler_params=_params("parallel", "parallel"),
        name="merge_mlp",
    )(x, attn_o, pooled, gates, mod, g_post_mix, g_pre_mlp, g_post_mlp,
      w_attn_up, w_pool_fold, w_out, w_ff1, w_ff2)


def _rope_tables(seq):
    t = np.arange(seq)
    rows = (t // GRID_W).astype(np.float64)
    cols = (t % GRID_W).astype(np.float64)
    freqs = ROPE_THETA ** (-np.arange(ROPE_FREQS, dtype=np.float64) / ROPE_FREQS)
    ang_row = rows[:, None] * freqs
    ang_col = cols[:, None] * freqs
    cos_r, sin_r, cos_c, sin_c = np.cos(ang_row), np.sin(ang_row), np.cos(ang_col), np.sin(ang_col)
    cos_t = np.concatenate([cos_r, cos_c, cos_r, cos_c], axis=-1).astype(np.float32)
    sin_t = np.concatenate([-sin_r, -sin_c, sin_r, sin_c], axis=-1).astype(np.float32)
    return jnp.asarray(cos_t), jnp.asarray(sin_t)


def kernel(x, c, ctx, c_ctx, w_mod, b_mod, g_pre_mix, g_post_mix, g_pre_mlp, g_post_mlp, w_in, b_gate, g_q, g_k, w_attn_up, w_pool_grp, pool_scale, w_pool_up, w_out, w_ff1, w_ff2):
    depth = w_mod.shape[0]
    assert depth == 1, "single-layer block"
    b, s, _ = x.shape
    row = lambda a: a[0].reshape(1, -1)

    n_cond = -(-(b + 1) // F32_SUBLANES) * F32_SUBLANES
    cond = jnp.concatenate([c, c_ctx[None, :], jnp.zeros((n_cond - b - 1, D_MODEL), F32)], axis=0)
    mod_all = _adaln(cond, w_mod[0], b_mod[0])
    mod = mod_all[:b].reshape(b, N_MOD, D_MODEL)
    mod_ctx = mod_all[b:b + 1].reshape(1, N_MOD, D_MODEL)

    w_in_b = _w_in_prep(w_in[0])
    g_q_p = _permute_head_dim(row(g_q))
    g_k_p = _permute_head_dim(row(g_k))
    cos_t, sin_t = _rope_tables(s)

    k_ctx, v_ctx = _ctx_kv(ctx, mod_ctx, row(g_pre_mix), w_in_b[:, K_OFF:POOL_OFF], g_k_p, samples=8)
    rope_tabs = _rope_fold(cos_t, sin_t, g_q_p, g_k_p)
    q, k_lat, v_lat, pooled, gates = _in_proj(
        x, mod, row(g_pre_mix), w_in_b, row(b_gate), rope_tabs, tm=1024, sub=256)
    attn_o = _attention(q, k_ctx, v_ctx, k_lat, v_lat, tq=2048, rows=128)
    w_pool_fold = _pool_fold(w_pool_grp[0], row(pool_scale), w_pool_up[0])
    return _merge_mlp(x, attn_o, pooled, gates, mod, row(g_post_mix), row(g_pre_mlp), row(g_post_mlp),
                      w_attn_up[0].astype(BF16), w_pool_fold, w_out[0].astype(BF16),
                      w_ff1[0].astype(BF16), w_ff2[0].astype(BF16), tm=1024, sub=256, ff_chunk=1024)
```

```python
import functools
import math

import jax
import jax.numpy as jnp
import numpy as np
from jax import lax
from jax.experimental import pallas as pl
from jax.experimental.pallas import tpu as pltpu

D_MODEL = 1024
GRID_W = 64
HEAD_DIM = 128
N_Q_HEADS = 8
N_KV_HEADS = 2
Q_GROUP = N_Q_HEADS // N_KV_HEADS
ATTN_WIDTH = N_Q_HEADS * HEAD_DIM
KV_WIDTH = N_KV_HEADS * HEAD_DIM
POOL_WINDOWS = (2, 4, 8, 16)
POOL_WIDTH = 512
POOL_GROUP_WIDTH = POOL_WIDTH // len(POOL_WINDOWS)
GATE_WIDTH = 2 * D_MODEL
K_OFF = ATTN_WIDTH
V_OFF = K_OFF + KV_WIDTH
POOL_OFF = V_OFF + KV_WIDTH
GATE_OFF = POOL_OFF + POOL_WIDTH
D_FF = 4 * D_MODEL
ROPE_THETA = 10000.0
ROPE_FREQS = HEAD_DIM // 4
ROPE_PARTNER = HEAD_DIM // 2
EPS = 1e-6
N_MOD = 6

F32_SUBLANES = 8
MXU_WIDTH = 256
VMEM_LIMIT_BYTES = 62 * 1024 * 1024

POOL_HALO = max(POOL_WINDOWS) // 2
assert POOL_HALO == F32_SUBLANES

Q_SCALE = (HEAD_DIM ** -0.5) * math.log2(math.e)

BF16 = jnp.bfloat16
F32 = jnp.float32


def _dot(a, b):
    return jnp.dot(a, b, preferred_element_type=F32)


def _dot_nt(a, b):
    return lax.dot_general(a, b, (((1,), (1,)), ((), ())), preferred_element_type=F32)


def _rms(x, gain):
    ms = jnp.mean(x * x, axis=-1, keepdims=True)
    return x * lax.rsqrt(ms + EPS) * gain


def _norm_mod(x, gain, shift, scale):
    return _rms(x, gain * (1.0 + scale)) + shift


def _permute_head_dim(a):
    lead = a.shape[:-1]
    blocks = a.reshape(*lead, -1, 2, 2, ROPE_FREQS)
    return jnp.swapaxes(blocks, -3, -2).reshape(a.shape)


_VMEM_RESIDENT = pl.BlockSpec(memory_space=pltpu.VMEM)


def _params(*sem):
    return pltpu.CompilerParams(dimension_semantics=sem, vmem_limit_bytes=VMEM_LIMIT_BYTES)


def _w_in_prep_kernel(w_ref, o_ref, *, bn):
    is_qk = pl.program_id(0) * bn < V_OFF

    @pl.when(is_qk)
    def _():
        block = lax.broadcasted_iota(jnp.int32, (w_ref.shape[0], HEAD_DIM), 1) // ROPE_FREQS
        for e in range(bn // HEAD_DIM):
            sl = slice(e * HEAD_DIM, (e + 1) * HEAD_DIM)
            t = w_ref[:, sl]
            swapped = jnp.where(block == 1, pltpu.roll(t, HEAD_DIM - ROPE_FREQS, 1),
                                jnp.where(block == 2, pltpu.roll(t, ROPE_FREQS, 1), t))
            o_ref[:, sl] = swapped.astype(BF16)

    @pl.when(jnp.logical_not(is_qk))
    def _():
        o_ref[...] = w_ref[...].astype(BF16)


def _w_in_prep(w_in):
    k, n = w_in.shape
    bn = MXU_WIDTH
    assert V_OFF % bn == 0
    return pl.pallas_call(
        functools.partial(_w_in_prep_kernel, bn=bn),
        grid=(n // bn,),
        in_specs=[pl.BlockSpec((k, bn), lambda j: (0, j))],
        out_specs=pl.BlockSpec((k, bn), lambda j: (0, j)),
        out_shape=jax.ShapeDtypeStruct((k, n), BF16),
        compiler_params=_params("parallel"),
        name="w_in_prep",
    )(w_in)


def _adaln_kernel(c_ref, w_ref, b_ref, o_ref):
    c = c_ref[...]
    a = (c * jax.nn.sigmoid(c)).astype(BF16)
    o_ref[...] = _dot(a, w_ref[...].astype(BF16)) + b_ref[...]


def _adaln(cond, w_mod, b_mod):
    rows = cond.shape[0]
    n_out = w_mod.shape[1]
    bn = D_MODEL
    return pl.pallas_call(
        _adaln_kernel,
        grid=(n_out // bn,),
        in_specs=[pl.BlockSpec((rows, D_MODEL), lambda j: (0, 0)),
                  pl.BlockSpec((D_MODEL, bn), lambda j: (0, j)),
                  pl.BlockSpec((1, bn), lambda j: (0, j))],
        out_specs=pl.BlockSpec((rows, bn), lambda j: (0, j)),
        out_shape=jax.ShapeDtypeStruct((rows, n_out), F32),
        compiler_params=_params("parallel"),
        name="adaln",
    )(cond, w_mod, b_mod.reshape(1, n_out))


def _ctx_kv_kernel(x_ref, mod_ref, gpre_ref, w_ref, gk_ref, k_ref, v_ref):
    for s in range(x_ref.shape[0]):
        h = _norm_mod(x_ref[s], gpre_ref[...], mod_ref[0, 0:1, :], mod_ref[0, 1:2, :]).astype(BF16)
        pk = _dot(h, w_ref[:, :KV_WIDTH])
        pv = _dot(h, w_ref[:, KV_WIDTH:])
        for e in range(N_KV_HEADS):
            sl = slice(e * HEAD_DIM, (e + 1) * HEAD_DIM)
            k_ref[s, e] = _rms(pk[:, sl], gk_ref[...]).astype(BF16)
            v_ref[s, e] = pv[:, sl].astype(BF16)


def _ctx_kv(ctx, mod_ctx, g_pre, w_kv, g_k, *, samples):
    b, n, _ = ctx.shape
    kv_shape = jax.ShapeDtypeStruct((b, N_KV_HEADS, n, HEAD_DIM), BF16)
    kv_spec = pl.BlockSpec((samples, N_KV_HEADS, n, HEAD_DIM), lambda i: (i, 0, 0, 0))
    return pl.pallas_call(
        _ctx_kv_kernel,
        grid=(b // samples,),
        in_specs=[pl.BlockSpec((samples, n, D_MODEL), lambda i: (i, 0, 0)),
                  pl.BlockSpec((1, N_MOD, D_MODEL), lambda i: (0, 0, 0)),
                  pl.BlockSpec((1, D_MODEL), lambda i: (0, 0)),
                  pl.BlockSpec((D_MODEL, 2 * KV_WIDTH), lambda i: (0, 0)),
                  pl.BlockSpec((1, HEAD_DIM), lambda i: (0, 0))],
        out_specs=[kv_spec, kv_spec],
        out_shape=[kv_shape, kv_shape],
        compiler_params=_params("parallel"),
        name="ctx_kv",
    )(ctx, mod_ctx, g_pre, w_kv, g_k)


def _pool_fold_kernel(wg_ref, ps_ref, wu_ref, o_ref):
    for g in range(len(POOL_WINDOWS)):
        sl = slice(g * POOL_GROUP_WIDTH, (g + 1) * POOL_GROUP_WIDTH)
        o_ref[sl, :] = jnp.dot(wg_ref[g] * ps_ref[:, sl], wu_ref[sl, :], precision=lax.Precision.HIGHEST,
                               preferred_element_type=F32).astype(BF16)


def _pool_fold(w_grp, pool_scale, w_pool_up):
    return pl.pallas_call(
        _pool_fold_kernel,
        out_shape=jax.ShapeDtypeStruct(w_pool_up.shape, BF16),
        compiler_params=pltpu.CompilerParams(vmem_limit_bytes=VMEM_LIMIT_BYTES),
        name="pool_fold",
    )(w_grp, pool_scale, w_pool_up)


def _rope_fold_kernel(cos_ref, sin_ref, gq_ref, gk_ref, cq_ref, sq_ref, ck_ref, sk_ref):
    cos = cos_ref[...]
    sin = sin_ref[...]
    for g_ref, c_ref, s_ref, logit_scale in ((gq_ref, cq_ref, sq_ref, Q_SCALE), (gk_ref, ck_ref, sk_ref, 1.0)):
        g = jnp.broadcast_to(g_ref[...], (F32_SUBLANES, HEAD_DIM)) * (logit_scale * math.sqrt(HEAD_DIM))
        g_partner = pltpu.roll(g, ROPE_PARTNER, 1)
        c_ref[...] = cos * g[0:1, :]
        s_ref[...] = sin * g_partner[0:1, :]


def _rope_fold(cos_t, sin_t, g_q, g_k):
    tab = jax.ShapeDtypeStruct(cos_t.shape, F32)
    return pl.pallas_call(
        _rope_fold_kernel,
        out_shape=[tab] * 4,
        compiler_params=pltpu.CompilerParams(vmem_limit_bytes=VMEM_LIMIT_BYTES),
        name="rope_fold",
    )(cos_t, sin_t, g_q, g_k)


def _in_proj_kernel(x_ref, xp_ref, xn_ref, mod_ref, gpre_ref, w_ref, bg_ref,
                    cq_ref, sq_ref, ck_ref, sk_ref, q_ref, k_ref, v_ref, pool_ref, gate_ref,
                    *, tm, sub, n_tiles, seq):
    i = pl.program_id(1)
    gpre = gpre_ref[...]
    shift = mod_ref[0, 0:1, :]
    scale = mod_ref[0, 1:2, :]
    subs = [slice(s * sub, (s + 1) * sub) for s in range(tm // sub)]
    hb = [_norm_mod(x_ref[0, rs, :], gpre, shift, scale).astype(BF16) for rs in subs]

    w_pool = w_ref[:, POOL_OFF:POOL_OFF + POOL_WIDTH]
    heads_per_dot = MXU_WIDTH // HEAD_DIM
    gate_chunk = MXU_WIDTH

    def norm_rope(t, cos_g, sin_g):
        inv = lax.rsqrt(jnp.sum(t * t, axis=-1, keepdims=True) + HEAD_DIM * EPS)
        return ((t * cos_g + pltpu.roll(t, ROPE_PARTNER, 1) * sin_g) * inv).astype(BF16)

    def pool_windows(u_sub):
        u_main = jnp.concatenate(u_sub, axis=0)
        x_halo = jnp.concatenate([xp_ref[0], xn_ref[0]], axis=0)
        u_halo = _dot(_norm_mod(x_halo, gpre, shift, scale).astype(BF16), w_pool)
        u_prev = jnp.where(i > 0, u_halo[:POOL_HALO], 0.0)
        u_next = jnp.where(i < n_tiles - 1, u_halo[POOL_HALO:], 0.0)
        u_ext = jnp.concatenate([u_prev, u_main, u_next], axis=0)
        n_ext = tm + 2 * POOL_HALO
        edge_iota = lax.broadcasted_iota(jnp.int32, (POOL_HALO, POOL_GROUP_WIDTH), 0)
        edge_t = (i * tm + edge_iota, i * tm + (tm - POOL_HALO) + edge_iota)
        for gi, win in enumerate(POOL_WINDOWS):
            sl = slice(gi * POOL_GROUP_WIDTH, (gi + 1) * POOL_GROUP_WIDTH)
            half = win // 2
            run = u_ext[:, sl]
            length = 1
            while length < half:
                run = run + pltpu.roll(run, n_ext - length, 0)
                length *= 2
            window = run + pltpu.roll(run, half, 0)
            window = window[POOL_HALO:POOL_HALO + tm]
            inv_cnt = [1.0 / (jnp.minimum(t + half, seq) - jnp.maximum(t - half, 0)).astype(F32)
                       for t in edge_t]
            inv_cnt = jnp.concatenate(
                [inv_cnt[0], jnp.full((tm - 2 * POOL_HALO, POOL_GROUP_WIDTH), 1.0 / win, F32), inv_cnt[1]],
                axis=0)
            pool_ref[0, :, sl] = (window * inv_cnt - u_main[:, sl]).astype(BF16)

    u_sub = []
    for si, (rs, h) in enumerate(zip(subs, hb)):
        u_sub.append(_dot(h, w_pool))
        ts = pl.ds(pl.multiple_of(i * tm + si * sub, sub), sub)
        cq, sq = cq_ref[ts, :], sq_ref[ts, :]
        for j in range(ATTN_WIDTH // MXU_WIDTH):
            pq = _dot(h, w_ref[:, j * MXU_WIDTH:(j + 1) * MXU_WIDTH])
            for e in range(heads_per_dot):
                t = pq[:, e * HEAD_DIM:(e + 1) * HEAD_DIM]
                q_ref[0, j * heads_per_dot + e, rs, :] = norm_rope(t, cq, sq)
        pk = _dot(h, w_ref[:, K_OFF:K_OFF + KV_WIDTH])
        pv = _dot(h, w_ref[:, V_OFF:V_OFF + KV_WIDTH])
        for e in range(N_KV_HEADS):
            sl = slice(e * HEAD_DIM, (e + 1) * HEAD_DIM)
            k_ref[0, e, rs, :] = norm_rope(pk[:, sl], ck_ref[ts, :], sk_ref[ts, :])
            v_ref[0, e, rs, :] = pv[:, sl].astype(BF16)
    pool_windows(u_sub)
    for rs, h in zip(subs, hb):
        for j in range(GATE_WIDTH // gate_chunk):
            sl = slice(j * gate_chunk, (j + 1) * gate_chunk)
            logits = _dot(h, w_ref[:, GATE_OFF + j * gate_chunk:GATE_OFF + (j + 1) * gate_chunk])
            gate_ref[0, rs, sl] = jax.nn.sigmoid(logits + bg_ref[:, sl]).astype(BF16)


def _in_proj(x, mod, g_pre, w_in, b_gate, rope_tabs, *, tm, sub):
    b, s, _ = x.shape
    n_tiles = s // tm
    halo_blocks_per_tile = tm // POOL_HALO
    last_halo_block = s // POOL_HALO - 1
    const2 = lambda bi, i: (0, 0)
    in_specs = [
        pl.BlockSpec((1, tm, D_MODEL), lambda bi, i: (bi, i, 0)),
        pl.BlockSpec((1, POOL_HALO, D_MODEL),
                     lambda bi, i: (bi, jnp.maximum(i * halo_blocks_per_tile - 1, 0), 0)),
        pl.BlockSpec((1, POOL_HALO, D_MODEL),
                     lambda bi, i: (bi, jnp.minimum((i + 1) * halo_blocks_per_tile, last_halo_block), 0)),
        pl.BlockSpec((1, N_MOD, D_MODEL), lambda bi, i: (bi, 0, 0)),
        pl.BlockSpec((1, D_MODEL), const2),
        _VMEM_RESIDENT,
        pl.BlockSpec((1, GATE_WIDTH), const2),
    ] + [_VMEM_RESIDENT] * len(rope_tabs)
    out_specs = [
        pl.BlockSpec((1, N_Q_HEADS, tm, HEAD_DIM), lambda bi, i: (bi, 0, i, 0)),
        pl.BlockSpec((1, N_KV_HEADS, tm, HEAD_DIM), lambda bi, i: (bi, 0, i, 0)),
        pl.BlockSpec((1, N_KV_HEADS, tm, HEAD_DIM), lambda bi, i: (bi, 0, i, 0)),
        pl.BlockSpec((1, tm, POOL_WIDTH), lambda bi, i: (bi, i, 0)),
        pl.BlockSpec((1, tm, GATE_WIDTH), lambda bi, i: (bi, i, 0)),
    ]
    out_shape = [
        jax.ShapeDtypeStruct((b, N_Q_HEADS, s, HEAD_DIM), BF16),
        jax.ShapeDtypeStruct((b, N_KV_HEADS, s, HEAD_DIM), BF16),
        jax.ShapeDtypeStruct((b, N_KV_HEADS, s, HEAD_DIM), BF16),
        jax.ShapeDtypeStruct((b, s, POOL_WIDTH), BF16),
        jax.ShapeDtypeStruct((b, s, GATE_WIDTH), BF16),
    ]
    return pl.pallas_call(
        functools.partial(_in_proj_kernel, tm=tm, sub=sub, n_tiles=n_tiles, seq=s),
        grid=(b, n_tiles),
        in_specs=in_specs,
        out_specs=out_specs,
        out_shape=out_shape,
        compiler_params=_params("parallel", "parallel"),
        name="in_proj",
    )(x, x, x, mod, g_pre, w_in, b_gate, *rope_tabs)


def _attention_kernel(q_ref, kc_ref, vc_ref, kl_ref, vl_ref, o_ref, *, tq, rows):
    kc = kc_ref[0, 0]
    kl = kl_ref[0, 0]
    ones_cols = lambda n: jnp.ones((n, MXU_WIDTH - HEAD_DIM), BF16)
    vc = jnp.concatenate([vc_ref[0, 0], ones_cols(kc.shape[0])], axis=-1)
    vl = jnp.concatenate([vl_ref[0, 0], ones_cols(kl.shape[0])], axis=-1)
    for h in range(Q_GROUP):
        for r in range(tq // rows):
            rs = slice(r * rows, (r + 1) * rows)
            q = q_ref[0, 0, h, rs, :]
            s_c = _dot_nt(q, kc)
            s_l = _dot_nt(q, kl)
            m = jnp.maximum(jnp.max(s_c, axis=-1, keepdims=True),
                            jnp.max(s_l, axis=-1, keepdims=True))
            p_c = jnp.exp2(s_c - m).astype(BF16)
            p_l = jnp.exp2(s_l - m).astype(BF16)
            o = _dot(p_c, vc) + _dot(p_l, vl)
            o_ref[0, rs, h * HEAD_DIM:(h + 1) * HEAD_DIM] = (
                o[:, :HEAD_DIM] / o[:, HEAD_DIM:]).astype(BF16)


def _attention(q, k_ctx, v_ctx, k_lat, v_lat, *, tq, rows):
    b, _, s, _ = q.shape
    n_ctx = k_ctx.shape[2]
    qg = q.reshape(b, N_KV_HEADS, Q_GROUP, s, HEAD_DIM)
    kv_idx = lambda bi, kh, i: (bi, kh, 0, 0)
    return pl.pallas_call(
        functools.partial(_attention_kernel, tq=tq, rows=rows),
        grid=(b, N_KV_HEADS, s // tq),
        in_specs=[pl.BlockSpec((1, 1, Q_GROUP, tq, HEAD_DIM), lambda bi, kh, i: (bi, kh, 0, i, 0)),
                  pl.BlockSpec((1, 1, n_ctx, HEAD_DIM), kv_idx),
                  pl.BlockSpec((1, 1, n_ctx, HEAD_DIM), kv_idx),
                  pl.BlockSpec((1, 1, s, HEAD_DIM), kv_idx),
                  pl.BlockSpec((1, 1, s, HEAD_DIM), kv_idx)],
        out_specs=pl.BlockSpec((1, tq, Q_GROUP * HEAD_DIM), lambda bi, kh, i: (bi, i, kh)),
        out_shape=jax.ShapeDtypeStruct((b, s, ATTN_WIDTH), BF16),
        compiler_params=_params("parallel", "parallel", "parallel"),
        name="attention",
    )(qg, k_ctx, v_ctx, k_lat, v_lat)


def _merge_mlp_kernel(x_ref, a_ref, p_ref, g_ref, mod_ref, gmix_ref, gpre_ref, gpost_ref,
                      wa_ref, wp_ref, wo_ref, w1_ref, w2_ref, o_ref, *, tm, sub, ff_chunk):
    subs = [slice(s * sub, (s + 1) * sub) for s in range(tm // sub)]
    x_mid = []
    for rs in subs:
        ya = _dot(a_ref[0, rs, :], wa_ref[...]).astype(BF16)
        yp = _dot(p_ref[0, rs, :], wp_ref[...]).astype(BF16)
        z = g_ref[0, rs, :D_MODEL] * ya + g_ref[0, rs, D_MODEL:] * yp
        y = _dot(z, wo_ref[...])
        x_mid.append(x_ref[0, rs, :] + _rms(y, gmix_ref[...] * mod_ref[0, 2:3, :]))
    for rs, xs in zip(subs, x_mid):
        h = _norm_mod(xs, gpre_ref[...], mod_ref[0, 3:4, :], mod_ref[0, 4:5, :]).astype(BF16)
        acc = None
        for j in range(D_FF // ff_chunk):
            sl = slice(j * ff_chunk, (j + 1) * ff_chunk)
            t = jnp.maximum(_dot(h, w1_ref[:, sl]), 0.0)
            part = _dot((t * t).astype(BF16), w2_ref[sl, :])
            acc = part if acc is None else acc + part
        o_ref[0, rs, :] = xs + _rms(acc, gpost_ref[...] * mod_ref[0, 5:6, :])


def _merge_mlp(x, attn_o, pooled, gates, mod, g_post_mix, g_pre_mlp, g_post_mlp,
               w_attn_up, w_pool_fold, w_out, w_ff1, w_ff2, *, tm, sub, ff_chunk):
    b, s, _ = x.shape
    tile = lambda width: pl.BlockSpec((1, tm, width), lambda bi, i: (bi, i, 0))
    const2 = lambda bi, i: (0, 0)
    gain = pl.BlockSpec((1, D_MODEL), const2)
    return pl.pallas_call(
        functools.partial(_merge_mlp_kernel, tm=tm, sub=sub, ff_chunk=ff_chunk),
        grid=(b, s // tm),
        in_specs=[tile(D_MODEL), tile(ATTN_WIDTH), tile(POOL_WIDTH), tile(GATE_WIDTH),
                  pl.BlockSpec((1, N_MOD, D_MODEL), lambda bi, i: (bi, 0, 0)),
                  gain, gain, gain] + [_VMEM_RESIDENT] * 5,
        out_specs=tile(D_MODEL),
        out_shape=jax.ShapeDtypeStruct(x.shape, F32),
        compiler_params=_params("parallel", "parallel"),
        name="merge_mlp",
    )(x, attn_o, pooled, gates, mod, g_post_mix, g_pre_mlp, g_post_mlp,
      w_attn_up, w_pool_fold, w_out, w_ff1, w_ff2)


def _rope_tables(seq):
    t = np.arange(seq)
    rows = (t // GRID_W).astype(np.float64)
    cols = (t % GRID_W).astype(np.float64)
    freqs = ROPE_THETA ** (-np.arange(ROPE_FREQS, dtype=np.float64) / ROPE_FREQS)
    ang_row = rows[:, None] * freqs
    ang_col = cols[:, None] * freqs
    cos_r, sin_r, cos_c, sin_c = np.cos(ang_row), np.sin(ang_row), np.cos(ang_col), np.sin(ang_col)
    cos_t = np.concatenate([cos_r, cos_c, cos_r, cos_c], axis=-1).astype(np.float32)
    sin_t = np.concatenate([-sin_r, -sin_c, sin_r, sin_c], axis=-1).astype(np.float32)
    return jnp.asarray(cos_t), jnp.asarray(sin_t)


def kernel(x, c, ctx, c_ctx, w_mod, b_mod, g_pre_mix, g_post_mix, g_pre_mlp, g_post_mlp, w_in, b_gate, g_q, g_k, w_attn_up, w_pool_grp, pool_scale, w_pool_up, w_out, w_ff1, w_ff2):
    depth = w_mod.shape[0]
    assert depth == 1, "single-layer block"
    b, s, _ = x.shape
    row = lambda a: a[0].reshape(1, -1)

    n_cond = -(-(b + 1) // F32_SUBLANES) * F32_SUBLANES
    cond = jnp.concatenate([c, c_ctx[None, :], jnp.zeros((n_cond - b - 1, D_MODEL), F32)], axis=0)
    mod_all = _adaln(cond, w_mod[0], b_mod[0])
    mod = mod_all[:b].reshape(b, N_MOD, D_MODEL)
    mod_ctx = mod_all[b:b + 1].reshape(1, N_MOD, D_MODEL)

    w_in_b = _w_in_prep(w_in[0])
    g_q_p = _permute_head_dim(row(g_q))
    g_k_p = _permute_head_dim(row(g_k))
    cos_t, sin_t = _rope_tables(s)

    k_ctx, v_ctx = _ctx_kv(ctx, mod_ctx, row(g_pre_mix), w_in_b[:, K_OFF:POOL_OFF], g_k_p, samples=8)
    rope_tabs = _rope_fold(cos_t, sin_t, g_q_p, g_k_p)
    q, k_lat, v_lat, pooled, gates = _in_proj(
        x, mod, row(g_pre_mix), w_in_b, row(b_gate), rope_tabs, tm=1024, sub=256)
    attn_o = _attention(q, k_ctx, v_ctx, k_lat, v_lat, tq=2048, rows=128)
    w_pool_fold = _pool_fold(w_pool_grp[0], row(pool_scale), w_pool_up[0])
    return _merge_mlp(x, attn_o, pooled, gates, mod, row(g_post_mix), row(g_pre_mlp), row(g_post_mlp),
                      w_attn_up[0].astype(BF16), w_pool_fold, w_out[0].astype(BF16),
                      w_ff1[0].astype(BF16), w_ff2[0].astype(BF16), tm=1024, sub=256, ff_chunk=1024)
```
